```python
import math
import jax, jax.numpy as jnp
from jax import lax
import numpy as np

D_MODEL = 4096
BATCH = 8
SEQ = 2048
DEPTH = 4

MIX_WIDTH = D_MODEL
GLA_WIDTH = MIX_WIDTH // 2
SB_WIDTH = MIX_WIDTH - GLA_WIDTH
GLA_HEADS = 4
GLA_DV = GLA_WIDTH // GLA_HEADS
GLA_DK = GLA_DV // 2
GLA_KEY_WIDTH = GLA_HEADS * GLA_DK
GLA_GATE_RANK = 16
GLA_GATE_TEMP = 16.0
GLA_CHUNK = 64
SB_HEAD_DIM = 128
SB_HEADS = SB_WIDTH // SB_HEAD_DIM
SB_BLOCK = 128
FFN_HIDDEN = int(math.ceil((8 * D_MODEL / 3) / 256) * 256)
NORM_EPS = 1e-6

IN_SIZES = [GLA_KEY_WIDTH, GLA_KEY_WIDTH, GLA_WIDTH, GLA_WIDTH, GLA_GATE_RANK, SB_WIDTH, SB_WIDTH, SB_WIDTH]
IN_WIDTH = int(sum(IN_SIZES))
IN_SPLITS = [int(v) for v in np.cumsum(IN_SIZES)[:-1]]

kernel_name = "hybrid_gla_stickbreaking_trunk"


def rms_norm(x, g):
    xf = x.astype(jnp.float32)
    return xf * lax.rsqrt(jnp.mean(xf * xf, axis=-1, keepdims=True) + NORM_EPS) * g.astype(jnp.float32)


def gla_mixer(q, k, v, log_alpha):
    B, T, H, dk = q.shape
    dv = v.shape[-1]
    n = T // GLA_CHUNK

    def to_chunks(a):
        return a.reshape(B, n, GLA_CHUNK, H, a.shape[-1]).transpose(1, 0, 3, 2, 4)

    q, k, v, log_alpha = map(to_chunks, (q * dk ** -0.5, k, v, log_alpha))
    b = jnp.cumsum(log_alpha, axis=3)
    b_last = b[:, :, :, -1:, :]
    q_dec = q * jnp.exp(b)
    k_intra = k * jnp.exp(-b)
    k_state = k * jnp.exp(b_last - b)

    causal = jnp.tril(jnp.ones((GLA_CHUNK, GLA_CHUNK), dtype=bool))
    scores = jnp.where(causal, jnp.einsum('nbhid,nbhjd->nbhij', q_dec, k_intra), 0.0)
    o_intra = jnp.einsum('nbhij,nbhjv->nbhiv', scores, v)

    def step(S, inp):
        qd, ks, vv, dl = inp
        o = jnp.einsum('bhid,bhdv->bhiv', qd, S)
        S = S * jnp.exp(dl)[..., None] + jnp.einsum('bhjd,bhjv->bhdv', ks, vv)
        return S, o

    S0 = jnp.zeros((B, H, dk, dv), jnp.float32)
    _, o_inter = lax.scan(step, S0, (q_dec, k_state, v, b_last[:, :, :, 0, :]))
    o = o_intra + o_inter
    return o.transpose(1, 0, 3, 2, 4).reshape(B, T, H, dv)


def stick_breaking(q, k, v):
    B, T, H, d = q.shape
    scale = d ** -0.5
    outs = []
    for i in range(T // SB_BLOCK):
        start, end = i * SB_BLOCK, (i + 1) * SB_BLOCK
        z = jnp.einsum('bqhd,bkhd->bhqk', q[:, start:end], k[:, :end]) * scale
        t_idx = start + jnp.arange(SB_BLOCK)[:, None]
        s_idx = jnp.arange(end)[None, :]
        mask = s_idx < t_idx
        log_keep = jnp.where(mask, jax.nn.log_sigmoid(-z), 0.0)
        suffix = lax.cumsum(log_keep, axis=3, reverse=True) - log_keep
        w = jnp.where(mask, jnp.exp(jax.nn.log_sigmoid(z) + suffix), 0.0)
        outs.append(jnp.einsum('bhqk,bkhd->bqhd', w, v[:, :end]))
    return jnp.concatenate(outs, axis=1)


def setup_inputs(seed: int = 0) -> dict:
    key = jax.random.key(seed)
    ks = jax.random.split(key, 14)
    f32 = jnp.float32
    nrm = lambda k, shape, s: jax.random.normal(k, shape, f32) * s
    return {
        "x": jax.random.normal(ks[0], (BATCH, SEQ, D_MODEL), f32),
        "norm_mix": 1.0 + nrm(ks[1], (DEPTH, D_MODEL), 0.02),
        "w_in": nrm(ks[2], (DEPTH, D_MODEL, IN_WIDTH), D_MODEL ** -0.5),
        "w_gate_up": nrm(ks[3], (DEPTH, GLA_GATE_RANK, GLA_KEY_WIDTH), GLA_GATE_RANK ** -0.5),
        "b_gate": nrm(ks[4], (DEPTH, GLA_KEY_WIDTH), 0.1),
        "gla_o_norm": 1.0 + nrm(ks[5], (DEPTH, GLA_DV), 0.02),
        "sb_q_norm": 1.0 + nrm(ks[6], (DEPTH, SB_HEAD_DIM), 0.02),
        "sb_k_norm": 1.0 + nrm(ks[7], (DEPTH, SB_HEAD_DIM), 0.02),
        "w_out": nrm(ks[8], (DEPTH, MIX_WIDTH, D_MODEL), MIX_WIDTH ** -0.5),
        "norm_ffn": 1.0 + nrm(ks[9], (DEPTH, D_MODEL), 0.02),
        "w_ffn_gate": nrm(ks[10], (DEPTH, D_MODEL, FFN_HIDDEN), D_MODEL ** -0.5),
        "w_ffn_up": nrm(ks[11], (DEPTH, D_MODEL, FFN_HIDDEN), D_MODEL ** -0.5),
        "w_ffn_down": nrm(ks[12], (DEPTH, FFN_HIDDEN, D_MODEL), FFN_HIDDEN ** -0.5),
    }


def reference(x, norm_mix, w_in, w_gate_up, b_gate, gla_o_norm, sb_q_norm, sb_k_norm,
              w_out, norm_ffn, w_ffn_gate, w_ffn_up, w_ffn_down):
    out_dtype = x.dtype
    B, T, _ = x.shape
    x = x.astype(jnp.float32)
    for l in range(DEPTH):
        h = rms_norm(x, norm_mix[l])
        proj = jnp.einsum('btd,de->bte', h, w_in[l].astype(jnp.float32))
        g_q, g_k, g_v, g_r, g_lr, s_q, s_k, s_v = jnp.split(proj, IN_SPLITS, axis=-1)

        gate_logits = jnp.einsum('btr,rk->btk', g_lr, w_gate_up[l].astype(jnp.float32)) + b_gate[l]
        log_alpha = jax.nn.log_sigmoid(gate_logits) / GLA_GATE_TEMP
        o_a = gla_mixer(g_q.reshape(B, T, GLA_HEADS, GLA_DK),
                        g_k.reshape(B, T, GLA_HEADS, GLA_DK),
                        g_v.reshape(B, T, GLA_HEADS, GLA_DV),
                        log_alpha.reshape(B, T, GLA_HEADS, GLA_DK))
        o_a = rms_norm(o_a, gla_o_norm[l]).reshape(B, T, GLA_WIDTH) * jax.nn.silu(g_r)

        qb = rms_norm(s_q.reshape(B, T, SB_HEADS, SB_HEAD_DIM), sb_q_norm[l])
        kb = rms_norm(s_k.reshape(B, T, SB_HEADS, SB_HEAD_DIM), sb_k_norm[l])
        o_b = stick_breaking(qb, kb, s_v.reshape(B, T, SB_HEADS, SB_HEAD_DIM)).reshape(B, T, SB_WIDTH)

        mixed = jnp.concatenate([o_a, o_b], axis=-1)
        x = x + jnp.einsum('bte,ed->btd', mixed, w_out[l].astype(jnp.float32))

        h = rms_norm(x, norm_ffn[l])
        u = jax.nn.silu(jnp.einsum('btd,df->btf', h, w_ffn_gate[l].astype(jnp.float32))) \
            * jnp.einsum('btd,df->btf', h, w_ffn_up[l].astype(jnp.float32))
        x = x + jnp.einsum('btf,fd->btd', u, w_ffn_down[l].astype(jnp.float32))
    return x.astype(out_dtype)
```

```python
import functools

import jax
import jax.numpy as jnp
from jax import lax
from jax.experimental import pallas as pl
from jax.experimental.pallas import tpu as pltpu

F32 = jnp.float32
BF16 = jnp.bfloat16

LANES = 128
V7X_VMEM_LIMIT_BYTES = 56 * 1024 * 1024

NORM_EPS = 1e-6
GLA_HEADS = 4
GLA_GATE_RANK = 16
GLA_GATE_TEMP = 16.0
GLA_CHUNK = 64
SB_HEAD_DIM = 128


def _params(*sem):
    return pltpu.CompilerParams(dimension_semantics=sem,
                                vmem_limit_bytes=V7X_VMEM_LIMIT_BYTES)


def _dot(a, b):
    return jnp.dot(a, b, preferred_element_type=F32)


def _dot_nt(a, b):
    return lax.dot_general(a, b, (((1,), (1,)), ((), ())), preferred_element_type=F32)


def _dot_tn(a, b):
    return lax.dot_general(a, b, (((0,), (0,)), ((), ())), preferred_element_type=F32)


def _split_bf16(a):
    hi = a.astype(BF16)
    lo = (a - hi.astype(F32)).astype(BF16)
    return hi, lo


def _softplus(z):
    return jnp.maximum(z, 0.0) + jnp.log(1.0 + jnp.exp(-jnp.abs(z)))


def _rms_rows(x, g):
    return x * lax.rsqrt(jnp.mean(x * x, axis=-1, keepdims=True) + NORM_EPS) * g


def _rmsnorm_body(x_ref, g_ref, o_ref):
    o_ref[...] = _rms_rows(x_ref[...], g_ref[...]).astype(o_ref.dtype)


def _rmsnorm(x, g, tm):
    n, d = x.shape
    return pl.pallas_call(
        _rmsnorm_body,
        grid=(n // tm,),
        in_specs=[pl.BlockSpec((tm, d), lambda i: (i, 0)),
                  pl.BlockSpec((1, d), lambda i: (0, 0))],
        out_specs=pl.BlockSpec((tm, d), lambda i: (i, 0)),
        out_shape=jax.ShapeDtypeStruct((n, d), BF16),
        compiler_params=_params("parallel"),
        name="rmsnorm",
    )(x, g.reshape(1, d))


def _mm_body(a_ref, w_ref, o_ref):
    o_ref[...] = _dot(a_ref[...], w_ref[...]).astype(o_ref.dtype)


def _matmul(a, w, tm, tn, out_dtype, name):
    m, k = a.shape
    n = w.shape[1]
    return pl.pallas_call(
        _mm_body,
        grid=(m // tm, n // tn),
        in_specs=[pl.BlockSpec((tm, k), lambda i, j: (i, 0)),
                  pl.BlockSpec((k, tn), lambda i, j: (0, j))],
        out_specs=pl.BlockSpec((tm, tn), lambda i, j: (i, j)),
        out_shape=jax.ShapeDtypeStruct((m, n), out_dtype),
        compiler_params=_params("parallel", "arbitrary"),
        name=name,
    )(a, w)


def _mm2_res_body(a1_ref, a2_ref, w1_ref, w2_ref, r_ref, o_ref):
    o_ref[...] = r_ref[...] + (_dot(a1_ref[...], w1_ref[...]) + _dot(a2_ref[...], w2_ref[...]))


def _matmul2_residual(a1, a2, w1, w2, res, tm, tn):
    m, k1 = a1.shape
    k2 = a2.shape[1]
    n = w1.shape[1]
    return pl.pallas_call(
        _mm2_res_body,
        grid=(m // tm, n // tn),
        in_specs=[pl.BlockSpec((tm, k1), lambda i, j: (i, 0)),
                  pl.BlockSpec((tm, k2), lambda i, j: (i, 0)),
                  pl.BlockSpec((k1, tn), lambda i, j: (0, j)),
                  pl.BlockSpec((k2, tn), lambda i, j: (0, j)),
                  pl.BlockSpec((tm, tn), lambda i, j: (i, j))],
        out_specs=pl.BlockSpec((tm, tn), lambda i, j: (i, j)),
        out_shape=jax.ShapeDtypeStruct((m, n), F32),
        compiler_params=_params("parallel", "arbitrary"),
        name="out_proj",
    )(a1, a2, w1, w2, res)


def _swiglu_body(a_ref, wg_ref, wu_ref, o_ref):
    a = a_ref[...]
    g = _dot(a, wg_ref[...])
    u = _dot(a, wu_ref[...])
    o_ref[...] = (g / (1.0 + jnp.exp(-g)) * u).astype(o_ref.dtype)


def _swiglu(a, wg, wu, tm, tn):
    m, k = a.shape
    n = wg.shape[1]
    return pl.pallas_call(
        _swiglu_body,
        grid=(m // tm, n // tn),
        in_specs=[pl.BlockSpec((tm, k), lambda i, j: (i, 0)),
                  pl.BlockSpec((k, tn), lambda i, j: (0, j)),
                  pl.BlockSpec((k, tn), lambda i, j: (0, j))],
        out_specs=pl.BlockSpec((tm, tn), lambda i, j: (i, j)),
        out_shape=jax.ShapeDtypeStruct((m, n), BF16),
        compiler_params=_params("parallel", "arbitrary"),
        name="ffn_gate_up",
    )(a, wg, wu)


def _down_body(a_ref, w_ref, r_ref, o_ref):
    part = _dot(a_ref[...], w_ref[...])

    @pl.when(pl.program_id(2) == 0)
    def _():
        o_ref[...] = r_ref[...] + part

    @pl.when(pl.program_id(2) != 0)
    def _():
        o_ref[...] += part


def _down_residual(a, w, res, tm, tn, tk):
    m, k = a.shape
    n = w.shape[1]
    return pl.pallas_call(
        _down_body,
        grid=(m // tm, n // tn, k // tk),
        in_specs=[pl.BlockSpec((tm, tk), lambda i, j, kk: (i, kk)),
                  pl.BlockSpec((tk, tn), lambda i, j, kk: (kk, j)),
                  pl.BlockSpec((tm, tn), lambda i, j, kk: (i, j))],
        out_specs=pl.BlockSpec((tm, tn), lambda i, j, kk: (i, j)),
        out_shape=jax.ShapeDtypeStruct((m, n), F32),
        compiler_params=_params("parallel", "arbitrary", "arbitrary"),
        name="ffn_down",
    )(a, w, res)


def _gla_body(q_ref, k_ref, v_ref, r_ref, lr_ref, wup_ref, bg_ref, gn_ref, o_ref, st_ref,
              *, n_chunks, dk):
    @pl.when(pl.program_id(2) == 0)
    def _():
        st_ref[...] = jnp.zeros_like(st_ref)

    c = GLA_CHUNK
    row = lax.broadcasted_iota(jnp.int32, (c, c), 0)
    col = lax.broadcasted_iota(jnp.int32, (c, c), 1)
    causal = col <= row
    tri = causal.astype(BF16)
    scale = dk ** -0.5

    for ci in range(n_chunks):
        rows = pl.ds(ci * c, c)
        logits = _dot(lr_ref[rows, :].astype(BF16), wup_ref[...]) + bg_ref[...]
        log_alpha = -_softplus(-logits) / GLA_GATE_TEMP
        la_hi, la_lo = _split_bf16(log_alpha)
        b = _dot(tri, la_hi) + _dot(tri, la_lo)
        b_last = b[c - 1:c, :]
        q = q_ref[rows, :].astype(F32)
        k = k_ref[rows, :].astype(F32)
        v = v_ref[rows, :]
        q_dec = (q * scale * jnp.exp(b)).astype(BF16)
        k_intra = (k * jnp.exp(-b)).astype(BF16)
        k_state = (k * jnp.exp(b_last - b)).astype(BF16)

        scores = jnp.where(causal, _dot_nt(q_dec, k_intra), 0.0)
        st = st_ref[...]
        o = _dot(scores.astype(BF16), v) + _dot_nt(q_dec, st.astype(BF16))
        st_ref[...] = st * jnp.exp(b_last) + _dot_tn(v, k_state)

        r = r_ref[rows, :].astype(F32)
        o = _rms_rows(o, gn_ref[...]) * (r / (1.0 + jnp.exp(-r)))
        o_ref[rows, :] = o.astype(o_ref.dtype)


def _gla(proj, lr, wup, bg, gn, *, heads, dk, dv, rows):
    bsz, t, _ = proj.shape
    kw = heads * dk
    k_blk0 = kw // dk
    v_blk0 = (2 * kw) // dv
    r_blk0 = (2 * kw + heads * dv) // dv
    body = functools.partial(_gla_body, n_chunks=rows // GLA_CHUNK, dk=dk)
    return pl.pallas_call(
        body,
        grid=(bsz, heads, t // rows),
        in_specs=[
            pl.BlockSpec((None, rows, dk), lambda b, h, i: (b, i, h)),
            pl.BlockSpec((None, rows, dk), lambda b, h, i: (b, i, k_blk0 + h)),
            pl.BlockSpec((None, rows, dv), lambda b, h, i: (b, i, v_blk0 + h)),
            pl.BlockSpec((None, rows, dv), lambda b, h, i: (b, i, r_blk0 + h)),
            pl.BlockSpec((None, rows, LANES), lambda b, h, i: (b, i, 0)),
            pl.BlockSpec((LANES, dk), lambda b, h, i: (0, h)),
            pl.BlockSpec((1, dk), lambda b, h, i: (0, h)),
            pl.BlockSpec((1, dv), lambda b, h, i: (0, 0)),
        ],
        out_specs=pl.BlockSpec((None, rows, dv), lambda b, h, i: (b, i, h)),
        out_shape=jax.ShapeDtypeStruct((bsz, t, heads * dv), BF16),
        scratch_shapes=[pltpu.VMEM((dv, dk), F32)],
        compiler_params=_params("parallel", "parallel", "arbitrary"),
        name="gla",
    )(proj, proj, proj, proj, lr, wup, bg, gn)


def _sb_tile(qn, kn, v, carry, upper, scale, mask):
    z = _dot_nt(qn, kn) * scale
    sp = _softplus(z)
    log_keep = -sp
    if mask is not None:
        log_keep = jnp.where(mask, log_keep, 0.0)
    lk_hi, lk_lo = _split_bf16(log_keep)
    suffix = _dot(lk_hi, upper) + _dot(lk_lo, upper)
    w = jnp.exp(z - sp + suffix + carry)
    if mask is not None:
        w = jnp.where(mask, w, 0.0)
    out = _dot(w.astype(BF16), v)
    carry = carry + (suffix[:, 0:1] + log_keep[:, 0:1])
    return out, carry


def _sb_body(q_ref, k_ref, v_ref, gq_ref, gk_ref, o_ref, kn_ref, *, tq, d):
    qi = pl.program_id(2)

    @pl.when(qi == 0)
    def _():
        kn_ref[...] = _rms_rows(k_ref[...].astype(F32), gk_ref[...]).astype(kn_ref.dtype)

    scale = d ** -0.5
    qn = _rms_rows(q_ref[...].astype(F32), gq_ref[...]).astype(BF16)
    row = lax.broadcasted_iota(jnp.int32, (tq, tq), 0)
    col = lax.broadcasted_iota(jnp.int32, (tq, tq), 1)
    upper = (row > col).astype(BF16)

    diag = pl.ds(pl.multiple_of(qi * tq, tq), tq)
    acc, carry = _sb_tile(qn, kn_ref[diag, :], v_ref[diag, :],
                          jnp.zeros((tq, 1), F32), upper, scale, col < row)

    def step(n, state):
        acc, carry = state
        blk = pl.ds(pl.multiple_of((qi - 1 - n) * tq, tq), tq)
        out, carry = _sb_tile(qn, kn_ref[blk, :], v_ref[blk, :], carry, upper, scale, None)
        return acc + out, carry

    acc, _ = lax.fori_loop(0, qi, step, (acc, carry))
    o_ref[...] = acc.astype(o_ref.dtype)


def _stick_breaking(proj, gq, gk, *, heads, d, col0, tq):
    bsz, t, _ = proj.shape
    q0 = col0 // d
    k0 = q0 + heads
    v0 = k0 + heads
    body = functools.partial(_sb_body, tq=tq, d=d)
    return pl.pallas_call(
        body,
        grid=(bsz, heads, t // tq),
        in_specs=[
            pl.BlockSpec((None, tq, d), lambda b, h, i: (b, i, q0 + h)),
            pl.BlockSpec((None, t, d), lambda b, h, i: (b, 0, k0 + h)),
            pl.BlockSpec((None, t, d), lambda b, h, i: (b, 0, v0 + h)),
            pl.BlockSpec((1, d), lambda b, h, i: (0, 0)),
            pl.BlockSpec((1, d), lambda b, h, i: (0, 0)),
        ],
        out_specs=pl.BlockSpec((None, tq, d), lambda b, h, i: (b, i, h)),
        out_shape=jax.ShapeDtypeStruct((bsz, t, heads * d), BF16),
        scratch_shapes=[pltpu.VMEM((t, d), BF16)],
        compiler_params=_params("parallel", "parallel", "arbitrary"),
        name="stick_breaking",
    )(proj, proj, proj, gq, gk)


def _pad_to(a, axis, size):
    pad = [(0, 0)] * a.ndim
    pad[axis] = (0, size - a.shape[axis])
    return jnp.pad(a, pad)


def _round_up(n, m):
    return -(-n // m) * m


def kernel(x, norm_mix, w_in, w_gate_up, b_gate, gla_o_norm, sb_q_norm, sb_k_norm, w_out,
           norm_ffn, w_ffn_gate, w_ffn_up, w_ffn_down):
    bsz, t, d_model = x.shape
    depth = w_in.shape[0]
    n = bsz * t
    gla_kw = w_gate_up.shape[2]
    gla_dv = gla_o_norm.shape[1]
    gla_w = GLA_HEADS * gla_dv
    gla_dk = gla_kw // GLA_HEADS
    lr0 = 2 * gla_kw + 2 * gla_w
    sb_w = (w_in.shape[2] - lr0 - GLA_GATE_RANK) // 3
    sb_heads = sb_w // SB_HEAD_DIM
    f_hidden = w_ffn_gate.shape[2]
    f_pad = _round_up(f_hidden, 1024)

    tm = min(1024, n)
    out_dtype = x.dtype
    x = x.astype(F32).reshape(n, d_model)

    for l in range(depth):
        w_l = w_in[l]
        w_main = jnp.concatenate([w_l[:, :lr0], w_l[:, lr0 + GLA_GATE_RANK:]], axis=1).astype(BF16)
        w_lr = _pad_to(w_l[:, lr0:lr0 + GLA_GATE_RANK], 1, LANES).astype(BF16)
        w_up = _pad_to(w_gate_up[l], 0, LANES).astype(BF16)
        w_o = w_out[l].astype(BF16)
        w_g = _pad_to(w_ffn_gate[l], 1, f_pad).astype(BF16)
        w_u = _pad_to(w_ffn_up[l], 1, f_pad).astype(BF16)
        w_d = _pad_to(w_ffn_down[l], 0, f_pad).astype(BF16)

        h = _rmsnorm(x, norm_mix[l], min(512, n))
        proj = _matmul(h, w_main, tm, min(1024, w_main.shape[1]), BF16, "in_proj")
        lr = _matmul(h, w_lr, tm, LANES, F32, "in_proj_gate")
        proj = proj.reshape(bsz, t, -1)
        o_a = _gla(proj, lr.reshape(bsz, t, LANES), w_up, b_gate[l].reshape(1, -1),
                   gla_o_norm[l].reshape(1, -1), heads=GLA_HEADS, dk=gla_dk, dv=gla_dv,
                   rows=min(256, t))
        o_b = _stick_breaking(proj, sb_q_norm[l].reshape(1, -1), sb_k_norm[l].reshape(1, -1),
                              heads=sb_heads, d=SB_HEAD_DIM, col0=lr0, tq=min(256, t))
        x = _matmul2_residual(o_a.reshape(n, gla_w), o_b.reshape(n, sb_w),
                              w_o[:gla_w], w_o[gla_w:], x, tm, min(512, d_model))

        h = _rmsnorm(x, norm_ffn[l], min(512, n))
        u = _swiglu(h, w_g, w_u, tm, min(512, f_pad))
        x = _down_residual(u, w_d, x, tm, min(1024, d_model), f_pad // 4)
    return x.reshape(bsz, t, d_model).astype(out_dtype)
```

```python
import functools

import jax
import jax.numpy as jnp
from jax import lax
from jax.experimental import pallas as pl
from jax.experimental.pallas import tpu as pltpu

F32 = jnp.float32
BF16 = jnp.bfloat16

LANES = 128
V7X_VMEM_LIMIT_BYTES = 56 * 1024 * 1024

NORM_EPS = 1e-6
GLA_HEADS = 4
GLA_GATE_RANK = 16
GLA_GATE_TEMP = 16.0
GLA_CHUNK = 64
SB_HEAD_DIM = 128
SB_ZERO_WEIGHT_CARRY = 105.0


def _params(*sem):
    return pltpu.CompilerParams(dimension_semantics=sem,
                                vmem_limit_bytes=V7X_VMEM_LIMIT_BYTES)


def _dot(a, b):
    return jnp.dot(a, b, preferred_element_type=F32)


def _dot_nt(a, b):
    return lax.dot_general(a, b, (((1,), (1,)), ((), ())), preferred_element_type=F32)


def _dot_tn(a, b):
    return lax.dot_general(a, b, (((0,), (0,)), ((), ())), preferred_element_type=F32)


def _split_bf16(a):
    hi = a.astype(BF16)
    lo = (a - hi.astype(F32)).astype(BF16)
    return hi, lo


def _softplus(z):
    return jnp.maximum(z, 0.0) + jnp.log(1.0 + jnp.exp(-jnp.abs(z)))


def _rms_rows(x, g):
    return x * lax.rsqrt(jnp.mean(x * x, axis=-1, keepdims=True) + NORM_EPS) * g


def _rmsnorm_body(x_ref, g_ref, o_ref):
    o_ref[...] = _rms_rows(x_ref[...], g_ref[...]).astype(o_ref.dtype)


def _rmsnorm(x, g, tm):
    n, d = x.shape
    return pl.pallas_call(
        _rmsnorm_body,
        grid=(n // tm,),
        in_specs=[pl.BlockSpec((tm, d), lambda i: (i, 0)),
                  pl.BlockSpec((1, d), lambda i: (0, 0))],
        out_specs=pl.BlockSpec((tm, d), lambda i: (i, 0)),
        out_shape=jax.ShapeDtypeStruct((n, d), BF16),
        compiler_params=_params("parallel"),
        name="rmsnorm",
    )(x, g.reshape(1, d))


def _mm_body(a_ref, w_ref, o_ref):
    o_ref[...] = _dot(a_ref[...], w_ref[...]).astype(o_ref.dtype)


def _matmul(a, w, layer, tm, tn, out_dtype, name):
    m, k = a.shape
    n = w.shape[2]
    return pl.pallas_call(
        _mm_body,
        grid=(m // tm, n // tn),
        in_specs=[pl.BlockSpec((tm, k), lambda i, j: (i, 0)),
                  pl.BlockSpec((None, k, tn), lambda i, j: (layer, 0, j))],
        out_specs=pl.BlockSpec((tm, tn), lambda i, j: (i, j)),
        out_shape=jax.ShapeDtypeStruct((m, n), out_dtype),
        compiler_params=_params("parallel", "arbitrary"),
        name=name,
    )(a, w)


def _mm2_res_body(a1_ref, a2_ref, w1_ref, w2_ref, r_ref, o_ref):
    o_ref[...] = r_ref[...] + (_dot(a1_ref[...], w1_ref[...]) + _dot(a2_ref[...], w2_ref[...]))


def _matmul2_residual(a1, a2, w, layer, res, tm, tn):
    m, kh = a1.shape
    assert a2.shape == a1.shape and w.shape[1] == 2 * kh
    n = w.shape[2]
    return pl.pallas_call(
        _mm2_res_body,
        grid=(m // tm, n // tn),
        in_specs=[pl.BlockSpec((tm, kh), lambda i, j: (i, 0)),
                  pl.BlockSpec((tm, kh), lambda i, j: (i, 0)),
                  pl.BlockSpec((None, kh, tn), lambda i, j: (layer, 0, j)),
                  pl.BlockSpec((None, kh, tn), lambda i, j: (layer, 1, j)),
                  pl.BlockSpec((tm, tn), lambda i, j: (i, j))],
        out_specs=pl.BlockSpec((tm, tn), lambda i, j: (i, j)),
        out_shape=jax.ShapeDtypeStruct((m, n), F32),
        compiler_params=_params("parallel", "arbitrary"),
        name="out_proj",
    )(a1, a2, w, w, res)


def _swiglu_body(a_ref, wg_ref, wu_ref, o_ref):
    a = a_ref[...]
    g = _dot(a, wg_ref[...])
    u = _dot(a, wu_ref[...])
    o_ref[...] = (g / (1.0 + jnp.exp(-g)) * u).astype(o_ref.dtype)


def _swiglu(a, wg, wu, layer, tm, tn):
    m, k = a.shape
    n = wg.shape[2]
    return pl.pallas_call(
        _swiglu_body,
        grid=(m // tm, n // tn),
        in_specs=[pl.BlockSpec((tm, k), lambda i, j: (i, 0)),
                  pl.BlockSpec((None, k, tn), lambda i, j: (layer, 0, j)),
                  pl.BlockSpec((None, k, tn), lambda i, j: (layer, 0, j))],
        out_specs=pl.BlockSpec((tm, tn), lambda i, j: (i, j)),
        out_shape=jax.ShapeDtypeStruct((m, n), BF16),
        compiler_params=_params("parallel", "arbitrary"),
        name="ffn_gate_up",
    )(a, wg, wu)


def _down_body(a_ref, w_ref, r_ref, o_ref):
    part = _dot(a_ref[...], w_ref[...])

    @pl.when(pl.program_id(2) == 0)
    def _():
        o_ref[...] = r_ref[...] + part

    @pl.when(pl.program_id(2) != 0)
    def _():
        o_ref[...] += part


def _down_residual(a, w, layer, res, tm, tn, tk):
    m, k = a.shape
    n = w.shape[2]
    return pl.pallas_call(
        _down_body,
        grid=(m // tm, n // tn, k // tk),
        in_specs=[pl.BlockSpec((tm, tk), lambda i, j, kk: (i, kk)),
                  pl.BlockSpec((None, tk, tn), lambda i, j, kk: (layer, kk, j)),
                  pl.BlockSpec((tm, tn), lambda i, j, kk: (i, j))],
        out_specs=pl.BlockSpec((tm, tn), lambda i, j, kk: (i, j)),
        out_shape=jax.ShapeDtypeStruct((m, n), F32),
        compiler_params=_params("parallel", "arbitrary", "arbitrary"),
        name="ffn_down",
    )(a, w, res)


def _gla_body(q_ref, k_ref, v_ref, r_ref, lr_ref, wup_ref, bg_ref, gn_ref, o_ref, st_ref,
              *, n_chunks, heads, dk, dv):
    @pl.when(pl.program_id(1) == 0)
    def _():
        st_ref[...] = jnp.zeros_like(st_ref)

    c = GLA_CHUNK
    row = lax.broadcasted_iota(jnp.int32, (c, c), 0)
    col = lax.broadcasted_iota(jnp.int32, (c, c), 1)
    causal = col <= row
    tri = causal.astype(BF16)
    tri2 = jnp.concatenate([tri, tri], axis=1)
    scale = dk ** -0.5

    for ci in range(n_chunks):
        rows = pl.ds(ci * c, c)
        logits = _dot(lr_ref[rows, :].astype(BF16), wup_ref[...]) + bg_ref[...]
        log_alpha = -_softplus(-logits) / GLA_GATE_TEMP
        b_all = _dot(tri2, jnp.concatenate(_split_bf16(log_alpha), axis=0))
        for h in range(heads):
            ks = pl.ds(h * dk, dk)
            vs = pl.ds(h * dv, dv)
            b = b_all[:, h * dk:(h + 1) * dk]
            b_last = b[c - 1:c, :]
            q = q_ref[rows, ks].astype(F32)
            k = k_ref[rows, ks].astype(F32)
            v = v_ref[rows, vs]
            q_dec = (q * scale * jnp.exp(b)).astype(BF16)
            k_intra = (k * jnp.exp(-b)).astype(BF16)
            k_state = (k * jnp.exp(b_last - b)).astype(BF16)

            scores = jnp.where(causal, _dot_nt(q_dec, k_intra), 0.0)
            st = st_ref[h]
            o = _dot(scores.astype(BF16), v) + _dot_nt(q_dec, st.astype(BF16))
            st_ref[h] = st * jnp.exp(b_last) + _dot_tn(v, k_state)

            r = r_ref[rows, vs].astype(F32)
            o = _rms_rows(o, gn_ref[...]) * (r / (1.0 + jnp.exp(-r)))
            o_ref[rows, vs] = o.astype(o_ref.dtype)


def _gla(proj, lr, wup, bg, gn, *, heads, dk, dv, rows):
    bsz, t, _ = proj.shape
    kw = heads * dk
    vw = heads * dv
    assert (2 * kw) % vw == 0
    v_blk = (2 * kw) // vw
    body = functools.partial(_gla_body, n_chunks=rows // GLA_CHUNK, heads=heads, dk=dk, dv=dv)
    return pl.pallas_call(
        body,
        grid=(bsz, t // rows),
        in_specs=[
            pl.BlockSpec((None, rows, kw), lambda b, i: (b, i, 0)),
            pl.BlockSpec((None, rows, kw), lambda b, i: (b, i, 1)),
            pl.BlockSpec((None, rows, vw), lambda b, i: (b, i, v_blk)),
            pl.BlockSpec((None, rows, vw), lambda b, i: (b, i, v_blk + 1)),
            pl.BlockSpec((None, rows, LANES), lambda b, i: (b, i, 0)),
            pl.BlockSpec((LANES, kw), lambda b, i: (0, 0)),
            pl.BlockSpec((1, kw), lambda b, i: (0, 0)),
            pl.BlockSpec((1, dv), lambda b, i: (0, 0)),
        ],
        out_specs=pl.BlockSpec((None, rows, vw), lambda b, i: (b, i, 0)),
        out_shape=jax.ShapeDtypeStruct((bsz, t, vw), BF16),
        scratch_shapes=[pltpu.VMEM((heads, dv, dk), F32)],
        compiler_params=_params("parallel", "arbitrary"),
        name="gla",
    )(proj, proj, proj, proj, lr, wup, bg, gn)


def _sb_body(q_ref, k_ref, v_ref, gq_ref, gk_ref, o_ref, kn_ref, qn_ref, acc_ref, carry_ref,
             lt_ref, *, tq, d, group):
    qi = pl.program_id(2)
    heads = [pl.ds(g * d, d) for g in range(group)]
    sub = tq // d

    @pl.when(qi == 0)
    def _():
        for hs in heads:
            kn_ref[:, hs] = _rms_rows(k_ref[:, hs].astype(F32), gk_ref[...]).astype(kn_ref.dtype)
        j = lax.broadcasted_iota(jnp.int32, (2 * d, 2 * d), 0) % d
        s = lax.broadcasted_iota(jnp.int32, (2 * d, 2 * d), 1)
        lt_ref[...] = ((j > s) | (s >= d)).astype(lt_ref.dtype)

    scale = d ** -0.5
    for hs in heads:
        qn_ref[:, hs] = (_rms_rows(q_ref[:, hs].astype(F32), gq_ref[...]) * scale).astype(qn_ref.dtype)
    acc_ref[...] = jnp.zeros_like(acc_ref)
    carry_ref[...] = jnp.zeros_like(carry_ref)

    def key_block(base, masks):
        tiles = [(hs, s) for hs in heads for s in reversed(range(sub))]
        blk = {s: pl.ds(pl.multiple_of(base + s * d, d), d) for s in range(sub)}
        z = [_dot_nt(qn_ref[:, hs], kn_ref[blk[s], hs]) for hs, s in tiles]
        sp = [_softplus(zt) for zt in z]
        if masks is None:
            keep = sp
        else:
            keep = [jnp.where(masks[s], spt, 0.0) for spt, (_, s) in zip(sp, tiles)]
        sums = [_dot(jnp.concatenate(_split_bf16(kt), axis=1), lt_ref[...]) for kt in keep]
        low = None
        it = iter(range(len(tiles)))
        for hs in heads:
            carry = carry_ref[:, hs]
            out = None
            for s in reversed(range(sub)):
                t = next(it)
                w = jnp.exp(z[t] - sp[t] - sums[t][:, :d] - carry)
                if masks is not None:
                    w = jnp.where(masks[s], w, 0.0)
                part = _dot(w.astype(BF16), v_ref[blk[s], hs])
                out = part if out is None else out + part
                carry = carry + sums[t][:, d:]
            carry_ref[:, hs] = carry
            acc_ref[:, hs] += out
            low = jnp.min(carry) if low is None else jnp.minimum(low, jnp.min(carry))
        return low

    row = lax.broadcasted_iota(jnp.int32, (tq, d), 0)
    col = lax.broadcasted_iota(jnp.int32, (tq, d), 1)
    low = key_block(qi * tq, [col + s * d < row for s in range(sub)])

    def more(state):
        n, low = state
        return jnp.logical_and(n < qi, low < SB_ZERO_WEIGHT_CARRY)

    def step(state):
        n, _ = state
        return n + 1, key_block((qi - 1 - n) * tq, None)

    lax.while_loop(more, step, (jnp.int32(0), low))
    o_ref[...] = acc_ref[...].astype(o_ref.dtype)


def _stick_breaking(proj, gq, gk, *, heads, d, col0, tq, group):
    bsz, t, _ = proj.shape
    assert d == LANES and tq % d == 0
    gw = group * d
    q0 = col0 // gw
    k0 = q0 + heads // group
    v0 = k0 + heads // group
    body = functools.partial(_sb_body, tq=tq, d=d, group=group)
    return pl.pallas_call(
        body,
        grid=(bsz, heads // group, t // tq),
        in_specs=[
            pl.BlockSpec((None, tq, gw), lambda b, h, i: (b, i, q0 + h)),
            pl.BlockSpec((None, t, gw), lambda b, h, i: (b, 0, k0 + h)),
            pl.BlockSpec((None, t, gw), lambda b, h, i: (b, 0, v0 + h)),
            pl.BlockSpec((1, d), lambda b, h, i: (0, 0)),
            pl.BlockSpec((1, d), lambda b, h, i: (0, 0)),
        ],
        out_specs=pl.BlockSpec((None, tq, gw), lambda b, h, i: (b, i, h)),
        out_shape=jax.ShapeDtypeStruct((bsz, t, heads * d), BF16),
        scratch_shapes=[pltpu.VMEM((t, gw), BF16),
                        pltpu.VMEM((tq, gw), BF16),
                        pltpu.VMEM((tq, gw), F32),
                        pltpu.VMEM((tq, gw), F32),
                        pltpu.VMEM((2 * d, 2 * d), BF16)],
        compiler_params=_params("parallel", "parallel", "arbitrary"),
        name="stick_breaking",
    )(proj, proj, proj, gq, gk)


def _pad_to(a, axis, size):
    pad = [(0, 0)] * a.ndim
    pad[axis] = (0, size - a.shape[axis])
    return jnp.pad(a, pad)


def _round_up(n, m):
    return -(-n // m) * m


def kernel(x, norm_mix, w_in, w_gate_up, b_gate, gla_o_norm, sb_q_norm, sb_k_norm, w_out,
           norm_ffn, w_ffn_gate, w_ffn_up, w_ffn_down):
    bsz, t, d_model = x.shape
    depth = w_in.shape[0]
    n = bsz * t
    gla_kw = w_gate_up.shape[2]
    gla_dv = gla_o_norm.shape[1]
    gla_w = GLA_HEADS * gla_dv
    gla_dk = gla_kw // GLA_HEADS
    lr0 = 2 * gla_kw + 2 * gla_w
    sb_w = (w_in.shape[2] - lr0 - GLA_GATE_RANK) // 3
    sb_heads = sb_w // SB_HEAD_DIM
    f_hidden = w_ffn_gate.shape[2]
    f_pad = _round_up(f_hidden, 1024)

    tm = min(1024, n)
    out_dtype = x.dtype
    x = x.astype(F32).reshape(n, d_model)

    w_main = jnp.concatenate([w_in[:, :, :lr0], w_in[:, :, lr0 + GLA_GATE_RANK:]], axis=2).astype(BF16)
    w_lr = _pad_to(w_in[:, :, lr0:lr0 + GLA_GATE_RANK], 2, LANES).astype(BF16)
    w_up = _pad_to(w_gate_up, 1, LANES).astype(BF16)
    w_o = w_out.astype(BF16)
    w_g = _pad_to(w_ffn_gate, 2, f_pad).astype(BF16)
    w_u = _pad_to(w_ffn_up, 2, f_pad).astype(BF16)
    w_d = _pad_to(w_ffn_down, 1, f_pad).astype(BF16)

    for l in range(depth):
        h = _rmsnorm(x, norm_mix[l], min(512, n))
        proj = _matmul(h, w_main, l, tm, min(1024, w_main.shape[2]), BF16, "in_proj")
        lr = _matmul(h, w_lr, l, tm, LANES, F32, "in_proj_gate")
        proj = proj.reshape(bsz, t, -1)
        o_a = _gla(proj, lr.reshape(bsz, t, LANES), w_up[l], b_gate[l].reshape(1, -1),
                   gla_o_norm[l].reshape(1, -1), heads=GLA_HEADS, dk=gla_dk, dv=gla_dv,
                   rows=min(256, t))
        o_b = _stick_breaking(proj, sb_q_norm[l].reshape(1, -1), sb_k_norm[l].reshape(1, -1),
                              heads=sb_heads, d=SB_HEAD_DIM, col0=lr0, tq=min(256, t), group=4)
        x = _matmul2_residual(o_a.reshape(n, gla_w), o_b.reshape(n, sb_w), w_o, l, x,
                              tm, min(512, d_model))

        h = _rmsnorm(x, norm_ffn[l], min(512, n))
        u = _swiglu(h, w_g, w_u, l, tm, min(512, f_pad))
        x = _down_residual(u, w_d, l, x, tm, min(1024, d_model), f_pad // 4)
    return x.reshape(bsz, t, d_model).astype(out_dtype)
```

```python
import functools

import jax
import jax.numpy as jnp
from jax import lax
from jax.experimental import pallas as pl
from jax.experimental.pallas import tpu as pltpu

F32 = jnp.float32
BF16 = jnp.bfloat16

LANES = 128
V7X_VMEM_LIMIT_BYTES = 56 * 1024 * 1024

NORM_EPS = 1e-6
GLA_HEADS = 4
GLA_GATE_RANK = 16
GLA_GATE_TEMP = 16.0
GLA_CHUNK = 64
SB_HEAD_DIM = 128
SB_ZERO_WEIGHT_CARRY = 105.0


def _params(*sem):
    return pltpu.CompilerParams(dimension_semantics=sem,
                                vmem_limit_bytes=V7X_VMEM_LIMIT_BYTES)


def _dot(a, b):
    return jnp.dot(a, b, preferred_element_type=F32)


def _dot_nt(a, b):
    return lax.dot_general(a, b, (((1,), (1,)), ((), ())), preferred_element_type=F32)


def _dot_tn(a, b):
    return lax.dot_general(a, b, (((0,), (0,)), ((), ())), preferred_element_type=F32)


def _split_bf16(a):
    hi = a.astype(BF16)
    lo = (a - hi.astype(F32)).astype(BF16)
    return hi, lo


def _softplus(z):
    return jnp.maximum(z, 0.0) + jnp.log(1.0 + jnp.exp(-jnp.abs(z)))


def _rms_rows(x, g):
    return x * lax.rsqrt(jnp.mean(x * x, axis=-1, keepdims=True) + NORM_EPS) * g


def _row_scale(ss_ref, width):
    return lax.rsqrt(ss_ref[...] * (1.0 / width) + NORM_EPS)


def _emit_norm_inputs(x, g_ref, xg_ref, ss_ref, first):
    xg_ref[...] = (x * g_ref[...]).astype(xg_ref.dtype)
    part = jnp.sum(x * x, axis=-1, keepdims=True)

    @pl.when(first)
    def _():
        ss_ref[...] = part

    @pl.when(jnp.logical_not(first))
    def _():
        ss_ref[...] += part


def _norm_out(m, n, tm, tn, idx):
    specs = [pl.BlockSpec((tm, tn), idx), pl.BlockSpec((tm, 1), lambda i, *_: (i, 0))]
    shapes = [jax.ShapeDtypeStruct((m, n), BF16), jax.ShapeDtypeStruct((m, 1), F32)]
    return specs, shapes


def _prenorm_body(x_ref, g_ref, xg_ref, ss_ref):
    x = x_ref[...]
    xg_ref[...] = (x * g_ref[...]).astype(xg_ref.dtype)
    ss_ref[...] = jnp.sum(x * x, axis=-1, keepdims=True)


def _prenorm(x, g, tm):
    n, d = x.shape
    specs, shapes = _norm_out(n, d, tm, d, lambda i: (i, 0))
    return pl.pallas_call(
        _prenorm_body,
        grid=(n // tm,),
        in_specs=[pl.BlockSpec((tm, d), lambda i: (i, 0)),
                  pl.BlockSpec((1, d), lambda i: (0, 0))],
        out_specs=specs,
        out_shape=shapes,
        compiler_params=_params("parallel"),
        name="prenorm",
    )(x, g)


def _mm_body(a_ref, ss_ref, w_ref, o_ref, *, width):
    o_ref[...] = (_dot(a_ref[...], w_ref[...]) * _row_scale(ss_ref, width)).astype(o_ref.dtype)


def _matmul_scaled(a, ss, w, layer, col0, ncols, tm, tn, out_dtype, name):
    m, k = a.shape
    assert col0 % tn == 0 and ncols % tn == 0
    blk0 = col0 // tn
    return pl.pallas_call(
        functools.partial(_mm_body, width=k),
        grid=(m // tm, ncols // tn),
        in_specs=[pl.BlockSpec((tm, k), lambda i, j: (i, 0)),
                  pl.BlockSpec((tm, 1), lambda i, j: (i, 0)),
                  pl.BlockSpec((None, k, tn), lambda i, j: (layer, 0, blk0 + j))],
        out_specs=pl.BlockSpec((tm, tn), lambda i, j: (i, j)),
        out_shape=jax.ShapeDtypeStruct((m, ncols), out_dtype),
        compiler_params=_params("parallel", "arbitrary"),
        name=name,
    )(a, ss, w)


def _mm2_res_body(a1_ref, a2_ref, w1_ref, w2_ref, r_ref, g_ref, o_ref, xg_ref, ss_ref):
    x = r_ref[...] + (_dot(a1_ref[...], w1_ref[...]) + _dot(a2_ref[...], w2_ref[...]))
    o_ref[...] = x
    _emit_norm_inputs(x, g_ref, xg_ref, ss_ref, pl.program_id(1) == 0)


def _matmul2_residual(a1, a2, w, layer, res, g, tm, tn):
    m, kh = a1.shape
    assert a2.shape == a1.shape and w.shape[1] == 2 * kh
    n = w.shape[2]
    nspecs, nshapes = _norm_out(m, n, tm, tn, lambda i, j: (i, j))
    return pl.pallas_call(
        _mm2_res_body,
        grid=(m // tm, n // tn),
        in_specs=[pl.BlockSpec((tm, kh), lambda i, j: (i, 0)),
                  pl.BlockSpec((tm, kh), lambda i, j: (i, 0)),
                  pl.BlockSpec((None, kh, tn), lambda i, j: (layer, 0, j)),
                  pl.BlockSpec((None, kh, tn), lambda i, j: (layer, 1, j)),
                  pl.BlockSpec((tm, tn), lambda i, j: (i, j)),
                  pl.BlockSpec((1, tn), lambda i, j: (0, j))],
        out_specs=[pl.BlockSpec((tm, tn), lambda i, j: (i, j))] + nspecs,
        out_shape=[jax.ShapeDtypeStruct((m, n), F32)] + nshapes,
        compiler_params=_params("parallel", "arbitrary"),
        name="out_proj",
    )(a1, a2, w, w, res, g)


def _swiglu_body(a_ref, ss_ref, wg_ref, wu_ref, o_ref, *, width):
    a = a_ref[...]
    rs = _row_scale(ss_ref, width)
    g = _dot(a, wg_ref[...]) * rs
    u = _dot(a, wu_ref[...]) * rs
    o_ref[...] = (g / (1.0 + jnp.exp(-g)) * u).astype(o_ref.dtype)


def _swiglu(a, ss, wg, wu, layer, tm, tn):
    m, k = a.shape
    n = wg.shape[2]
    return pl.pallas_call(
        functools.partial(_swiglu_body, width=k),
        grid=(m // tm, n // tn),
        in_specs=[pl.BlockSpec((tm, k), lambda i, j: (i, 0)),
                  pl.BlockSpec((tm, 1), lambda i, j: (i, 0)),
                  pl.BlockSpec((None, k, tn), lambda i, j: (layer, 0, j)),
                  pl.BlockSpec((None, k, tn), lambda i, j: (layer, 0, j))],
        out_specs=pl.BlockSpec((tm, tn), lambda i, j: (i, j)),
        out_shape=jax.ShapeDtypeStruct((m, n), BF16),
        compiler_params=_params("parallel", "arbitrary"),
        name="ffn_gate_up",
    )(a, ss, wg, wu)


def _down_body(a_ref, w_ref, r_ref, o_ref):
    o_ref[...] = r_ref[...] + _dot(a_ref[...], w_ref[...])


def _down_norm_body(a_ref, w_ref, r_ref, g_ref, o_ref, xg_ref, ss_ref):
    x = r_ref[...] + _dot(a_ref[...], w_ref[...])
    o_ref[...] = x
    _emit_norm_inputs(x, g_ref, xg_ref, ss_ref, pl.program_id(1) == 0)


def _down_residual(a, w, layer, res, g, tm, tn):
    m = a.shape[0]
    k, n = w.shape[1:]
    in_specs = [pl.BlockSpec((tm, k), lambda i, j: (i, 0)),
                pl.BlockSpec((None, k, tn), lambda i, j: (layer, 0, j)),
                pl.BlockSpec((tm, tn), lambda i, j: (i, j))]
    out_specs = [pl.BlockSpec((tm, tn), lambda i, j: (i, j))]
    out_shape = [jax.ShapeDtypeStruct((m, n), F32)]
    args = [a, w, res]
    body = _down_body
    if g is not None:
        nspecs, nshapes = _norm_out(m, n, tm, tn, lambda i, j: (i, j))
        in_specs.append(pl.BlockSpec((1, tn), lambda i, j: (0, j)))
        out_specs += nspecs
        out_shape += nshapes
        args.append(g)
        body = _down_norm_body
    return pl.pallas_call(
        body,
        grid=(m // tm, n // tn),
        in_specs=in_specs,
        out_specs=out_specs,
        out_shape=out_shape,
        compiler_params=_params("parallel", "arbitrary"),
        name="ffn_down",
    )(*args)


def _gla_body(q_ref, k_ref, v_ref, r_ref, lr_ref, wup_ref, bg_ref, gn_ref, o_ref, st_ref,
              *, n_chunks, heads, dk, dv):
    @pl.when(pl.program_id(1) == 0)
    def _():
        st_ref[...] = jnp.zeros_like(st_ref)

    c = GLA_CHUNK
    row = lax.broadcasted_iota(jnp.int32, (c, c), 0)
    col = lax.broadcasted_iota(jnp.int32, (c, c), 1)
    causal = col <= row
    tri = causal.astype(BF16)
    tri2 = jnp.concatenate([tri, tri], axis=1)
    scale = dk ** -0.5

    for ci in range(n_chunks):
        rows = pl.ds(ci * c, c)
        logits = _dot(lr_ref[rows, :].astype(BF16), wup_ref[...]) + bg_ref[...]
        log_alpha = -_softplus(-logits) / GLA_GATE_TEMP
        b_all = _dot(tri2, jnp.concatenate(_split_bf16(log_alpha), axis=0))
        for h in range(heads):
            ks = pl.ds(h * dk, dk)
            vs = pl.ds(h * dv, dv)
            b = b_all[:, h * dk:(h + 1) * dk]
            b_last = b[c - 1:c, :]
            q = q_ref[rows, ks].astype(F32)
            k = k_ref[rows, ks].astype(F32)
            v = v_ref[rows, vs]
            q_dec = (q * scale * jnp.exp(b)).astype(BF16)
            k_intra = (k * jnp.exp(-b)).astype(BF16)
            k_state = (k * jnp.exp(b_last - b)).astype(BF16)

            scores = jnp.where(causal, _dot_nt(q_dec, k_intra), 0.0)
            st = st_ref[h]
            o = _dot(scores.astype(BF16), v) + _dot_nt(q_dec, st.astype(BF16))
            st_ref[h] = st * jnp.exp(b_last) + _dot_tn(v, k_state)

            r = r_ref[rows, vs].astype(F32)
            o = _rms_rows(o, gn_ref[...]) * (r / (1.0 + jnp.exp(-r)))
            o_ref[rows, vs] = o.astype(o_ref.dtype)


def _gla(proj, lr, wup, bg, gn, *, heads, dk, dv, rows):
    bsz, t, _ = proj.shape
    kw = heads * dk
    vw = heads * dv
    assert (2 * kw) % vw == 0
    v_blk = (2 * kw) // vw
    body = functools.partial(_gla_body, n_chunks=rows // GLA_CHUNK, heads=heads, dk=dk, dv=dv)
    return pl.pallas_call(
        body,
        grid=(bsz, t // rows),
        in_specs=[
            pl.BlockSpec((None, rows, kw), lambda b, i: (b, i, 0)),
            pl.BlockSpec((None, rows, kw), lambda b, i: (b, i, 1)),
            pl.BlockSpec((None, rows, vw), lambda b, i: (b, i, v_blk)),
            pl.BlockSpec((None, rows, vw), lambda b, i: (b, i, v_blk + 1)),
            pl.BlockSpec((None, rows, LANES), lambda b, i: (b, i, 0)),
            pl.BlockSpec((LANES, kw), lambda b, i: (0, 0)),
            pl.BlockSpec((1, kw), lambda b, i: (0, 0)),
            pl.BlockSpec((1, dv), lambda b, i: (0, 0)),
        ],
        out_specs=pl.BlockSpec((None, rows, vw), lambda b, i: (b, i, 0)),
        out_shape=jax.ShapeDtypeStruct((bsz, t, vw), BF16),
        scratch_shapes=[pltpu.VMEM((heads, dv, dk), F32)],
        compiler_params=_params("parallel", "arbitrary"),
        name="gla",
    )(proj, proj, proj, proj, lr, wup, bg, gn)


def _sb_body(q_ref, k_ref, v_ref, gq_ref, gk_ref, o_ref, kn_ref, qn_ref, acc_ref, carry_ref,
             lt_ref, *, tq, d, group):
    qi = pl.program_id(2)
    heads = [pl.ds(g * d, d) for g in range(group)]
    sub = tq // d

    @pl.when(qi == 0)
    def _():
        for hs in heads:
            kn_ref[:, hs] = _rms_rows(k_ref[:, hs].astype(F32), gk_ref[...]).astype(kn_ref.dtype)
        j = lax.broadcasted_iota(jnp.int32, (2 * d, 2 * d), 0) % d
        s = lax.broadcasted_iota(jnp.int32, (2 * d, 2 * d), 1)
        lt_ref[...] = ((j > s) | (s >= d)).astype(lt_ref.dtype)

    scale = d ** -0.5
    for hs in heads:
        qn_ref[:, hs] = (_rms_rows(q_ref[:, hs].astype(F32), gq_ref[...]) * scale).astype(qn_ref.dtype)
    acc_ref[...] = jnp.zeros_like(acc_ref)
    carry_ref[...] = jnp.zeros_like(carry_ref)

    def key_block(base, masks):
        tiles = [(hs, s) for hs in heads for s in reversed(range(sub))]
        blk = {s: pl.ds(pl.multiple_of(base + s * d, d), d) for s in range(sub)}
        z = [_dot_nt(qn_ref[:, hs], kn_ref[blk[s], hs]) for hs, s in tiles]
        sp = [_softplus(zt) for zt in z]
        if masks is None:
            keep = sp
        else:
            keep = [jnp.where(masks[s], spt, 0.0) for spt, (_, s) in zip(sp, tiles)]
        sums = [_dot(jnp.concatenate(_split_bf16(kt), axis=1), lt_ref[...]) for kt in keep]
        low = None
        it = iter(range(len(tiles)))
        for hs in heads:
            carry = carry_ref[:, hs]
            out = None
            for s in reversed(range(sub)):
                t = next(it)
                w = jnp.exp(z[t] - sp[t] - sums[t][:, :d] - carry)
                if masks is not None:
                    w = jnp.where(masks[s], w, 0.0)
                part = _dot(w.astype(BF16), v_ref[blk[s], hs])
                out = part if out is None else out + part
                carry = carry + sums[t][:, d:]
            carry_ref[:, hs] = carry
            acc_ref[:, hs] += out
            low = jnp.min(carry) if low is None else jnp.minimum(low, jnp.min(carry))
        return low

    row = lax.broadcasted_iota(jnp.int32, (tq, d), 0)
    col = lax.broadcasted_iota(jnp.int32, (tq, d), 1)
    low = key_block(qi * tq, [col + s * d < row for s in range(sub)])

    def more(state):
        n, low = state
        return jnp.logical_and(n < qi, low < SB_ZERO_WEIGHT_CARRY)

    def step(state):
        n, _ = state
        return n + 1, key_block((qi - 1 - n) * tq, None)

    lax.while_loop(more, step, (jnp.int32(0), low))
    o_ref[...] = acc_ref[...].astype(o_ref.dtype)


def _stick_breaking(proj, gq, gk, *, heads, d, col0, tq, group):
    bsz, t, _ = proj.shape
    assert d == LANES and tq % d == 0
    gw = group * d
    q0 = col0 // gw
    k0 = q0 + heads // group
    v0 = k0 + heads // group
    body = functools.partial(_sb_body, tq=tq, d=d, group=group)
    return pl.pallas_call(
        body,
        grid=(bsz, heads // group, t // tq),
        in_specs=[
            pl.BlockSpec((None, tq, gw), lambda b, h, i: (b, i, q0 + h)),
            pl.BlockSpec((None, t, gw), lambda b, h, i: (b, 0, k0 + h)),
            pl.BlockSpec((None, t, gw), lambda b, h, i: (b, 0, v0 + h)),
            pl.BlockSpec((1, d), lambda b, h, i: (0, 0)),
            pl.BlockSpec((1, d), lambda b, h, i: (0, 0)),
        ],
        out_specs=pl.BlockSpec((None, tq, gw), lambda b, h, i: (b, i, h)),
        out_shape=jax.ShapeDtypeStruct((bsz, t, heads * d), BF16),
        scratch_shapes=[pltpu.VMEM((t, gw), BF16),
                        pltpu.VMEM((tq, gw), BF16),
                        pltpu.VMEM((tq, gw), F32),
                        pltpu.VMEM((tq, gw), F32),
                        pltpu.VMEM((2 * d, 2 * d), BF16)],
        compiler_params=_params("parallel", "parallel", "arbitrary"),
        name="stick_breaking",
    )(proj, proj, proj, gq, gk)


def _pad_to(a, axis, size):
    pad = [(0, 0)] * a.ndim
    pad[axis] = (0, size - a.shape[axis])
    return jnp.pad(a, pad)


def _round_up(n, m):
    return -(-n // m) * m


def kernel(x, norm_mix, w_in, w_gate_up, b_gate, gla_o_norm, sb_q_norm, sb_k_norm, w_out,
           norm_ffn, w_ffn_gate, w_ffn_up, w_ffn_down):
    bsz, t, d_model = x.shape
    depth = w_in.shape[0]
    n = bsz * t
    gla_kw = w_gate_up.shape[2]
    gla_dv = gla_o_norm.shape[1]
    gla_w = GLA_HEADS * gla_dv
    gla_dk = gla_kw // GLA_HEADS
    lr0 = 2 * gla_kw + 2 * gla_w
    sb_w = (w_in.shape[2] - lr0 - GLA_GATE_RANK) // 3
    sb_heads = sb_w // SB_HEAD_DIM
    f_hidden = w_ffn_gate.shape[2]
    f_pad = _round_up(f_hidden, 1024)

    out_dtype = x.dtype
    x = x.astype(F32).reshape(n, d_model)

    w_in_bf = w_in.astype(BF16)
    w_sb = w_in_bf[:, :, lr0 + GLA_GATE_RANK:]
    w_up = _pad_to(w_gate_up, 1, LANES).astype(BF16)
    w_o = w_out.astype(BF16)
    w_g = _pad_to(w_ffn_gate, 2, f_pad).astype(BF16)
    w_u = _pad_to(w_ffn_up, 2, f_pad).astype(BF16)
    w_d = w_ffn_down.astype(BF16)
    g_mix = norm_mix.reshape(depth, 1, d_model)
    g_ffn = norm_ffn.reshape(depth, 1, d_model)

    tm = min(1024, n)
    xg, ss = _prenorm(x, g_mix[0], min(512, n))
    for l in range(depth):
        proj_gla = _matmul_scaled(xg, ss, w_in_bf, l, 0, lr0, tm, min(1024, lr0), BF16, "in_proj_gla")
        lr = _matmul_scaled(xg, ss, w_in_bf, l, lr0, LANES, tm, LANES, F32, "in_proj_gate")
        proj_sb = _matmul_scaled(xg, ss, w_sb, l, 0, 3 * sb_w, tm, min(1024, 3 * sb_w), BF16, "in_proj_sb")
        o_a = _gla(proj_gla.reshape(bsz, t, lr0), lr.reshape(bsz, t, LANES), w_up[l],
                   b_gate[l].reshape(1, -1), gla_o_norm[l].reshape(1, -1),
                   heads=GLA_HEADS, dk=gla_dk, dv=gla_dv, rows=min(256, t))
        o_b = _stick_breaking(proj_sb.reshape(bsz, t, 3 * sb_w), sb_q_norm[l].reshape(1, -1),
                              sb_k_norm[l].reshape(1, -1), heads=sb_heads, d=SB_HEAD_DIM, col0=0,
                              tq=min(256, t), group=4)
        x, xg, ss = _matmul2_residual(o_a.reshape(n, gla_w), o_b.reshape(n, sb_w), w_o, l, x,
                                      g_ffn[l], tm, min(512, d_model))

        u = _swiglu(xg, ss, w_g, w_u, l, tm, min(512, f_pad))
        if l + 1 < depth:
            x, xg, ss = _down_residual(u, w_d, l, x, g_mix[l + 1], min(512, n), min(512, d_model))
        else:
            (x,) = _down_residual(u, w_d, l, x, None, min(512, n), min(512, d_model))
    return x.reshape(bsz, t, d_model).astype(out_dtype)
```

```python
import functools

import jax
import jax.numpy as jnp
from jax import lax
from jax.experimental import pallas as pl
from jax.experimental.pallas import tpu as pltpu

F32 = jnp.float32
BF16 = jnp.bfloat16

LANES = 128
V7X_VMEM_LIMIT_BYTES = 56 * 1024 * 1024

NORM_EPS = 1e-6
GLA_HEADS = 4
GLA_GATE_RANK = 16
GLA_GATE_TEMP = 16.0
GLA_CHUNK = 64
SB_HEAD_DIM = 128
SB_ZERO_WEIGHT_CARRY = 105.0


def _params(*sem):
    return pltpu.CompilerParams(dimension_semantics=sem,
                                vmem_limit_bytes=V7X_VMEM_LIMIT_BYTES)


def _dot(a, b):
    return jnp.dot(a, b, preferred_element_type=F32)


def _dot_nt(a, b):
    return lax.dot_general(a, b, (((1,), (1,)), ((), ())), preferred_element_type=F32)


def _dot_tn(a, b):
    return lax.dot_general(a, b, (((0,), (0,)), ((), ())), preferred_element_type=F32)


def _split_bf16(a):
    hi = a.astype(BF16)
    lo = (a - hi.astype(F32)).astype(BF16)
    return hi, lo


def _softplus(z):
    return jnp.maximum(z, 0.0) + jnp.log(1.0 + jnp.exp(-jnp.abs(z)))


def _rms_rows(x, g):
    return x * lax.rsqrt(jnp.mean(x * x, axis=-1, keepdims=True) + NORM_EPS) * g


def _row_scale(ss_ref, width):
    return lax.rsqrt(ss_ref[...] * (1.0 / width) + NORM_EPS)


def _emit_norm_inputs(x, g_ref, xg_ref, ss_ref, first):
    xg_ref[...] = (x * g_ref[...]).astype(xg_ref.dtype)
    part = jnp.sum(x * x, axis=-1, keepdims=True)

    @pl.when(first)
    def _():
        ss_ref[...] = part

    @pl.when(jnp.logical_not(first))
    def _():
        ss_ref[...] += part


def _norm_out(m, n, tm, tn, idx):
    specs = [pl.BlockSpec((tm, tn), idx), pl.BlockSpec((tm, 1), lambda i, *_: (i, 0))]
    shapes = [jax.ShapeDtypeStruct((m, n), BF16), jax.ShapeDtypeStruct((m, 1), F32)]
    return specs, shapes


def _prenorm_body(x_ref, g_ref, xg_ref, ss_ref):
    x = x_ref[...]
    xg_ref[...] = (x * g_ref[...]).astype(xg_ref.dtype)
    ss_ref[...] = jnp.sum(x * x, axis=-1, keepdims=True)


def _prenorm(x, g, tm):
    n, d = x.shape
    specs, shapes = _norm_out(n, d, tm, d, lambda i: (i, 0))
    return pl.pallas_call(
        _prenorm_body,
        grid=(n // tm,),
        in_specs=[pl.BlockSpec((tm, d), lambda i: (i, 0)),
                  pl.BlockSpec((1, d), lambda i: (0, 0))],
        out_specs=specs,
        out_shape=shapes,
        compiler_params=_params("parallel"),
        name="prenorm",
    )(x, g)


def _mm_body(a_ref, ss_ref, w_ref, o_ref, *, width):
    o_ref[...] = (_dot(a_ref[...], w_ref[...]) * _row_scale(ss_ref, width)).astype(o_ref.dtype)


def _matmul_scaled(a, ss, w, layer, col0, ncols, tm, tn, out_dtype, name):
    m, k = a.shape
    assert col0 % tn == 0 and ncols % tn == 0
    blk0 = col0 // tn
    return pl.pallas_call(
        functools.partial(_mm_body, width=k),
        grid=(m // tm, ncols // tn),
        in_specs=[pl.BlockSpec((tm, k), lambda i, j: (i, 0)),
                  pl.BlockSpec((tm, 1), lambda i, j: (i, 0)),
                  pl.BlockSpec((None, k, tn), lambda i, j: (layer, 0, blk0 + j))],
        out_specs=pl.BlockSpec((tm, tn), lambda i, j: (i, j)),
        out_shape=jax.ShapeDtypeStruct((m, ncols), out_dtype),
        compiler_params=_params("parallel", "arbitrary"),
        name=name,
    )(a, ss, w)


def _mm2_res_body(a1_ref, a2_ref, w1_ref, w2_ref, r_ref, g_ref, o_ref, xg_ref, ss_ref):
    x = r_ref[...] + (_dot(a1_ref[...], w1_ref[...]) + _dot(a2_ref[...], w2_ref[...]))
    o_ref[...] = x
    _emit_norm_inputs(x, g_ref, xg_ref, ss_ref, pl.program_id(1) == 0)


def _matmul2_residual(a1, a2, w, layer, res, g, tm, tn):
    m, kh = a1.shape
    assert a2.shape == a1.shape and w.shape[1] == 2 * kh
    n = w.shape[2]
    nspecs, nshapes = _norm_out(m, n, tm, tn, lambda i, j: (i, j))
    return pl.pallas_call(
        _mm2_res_body,
        grid=(m // tm, n // tn),
        in_specs=[pl.BlockSpec((tm, kh), lambda i, j: (i, 0)),
                  pl.BlockSpec((tm, kh), lambda i, j: (i, 0)),
                  pl.BlockSpec((None, kh, tn), lambda i, j: (layer, 0, j)),
                  pl.BlockSpec((None, kh, tn), lambda i, j: (layer, 1, j)),
                  pl.BlockSpec((tm, tn), lambda i, j: (i, j)),
                  pl.BlockSpec((1, tn), lambda i, j: (0, j))],
        out_specs=[pl.BlockSpec((tm, tn), lambda i, j: (i, j))] + nspecs,
        out_shape=[jax.ShapeDtypeStruct((m, n), F32)] + nshapes,
        compiler_params=_params("parallel", "arbitrary"),
        name="out_proj",
    )(a1, a2, w, w, res, g)


def _swiglu_body(a_ref, ss_ref, wg_ref, wu_ref, *rest, width):
    o_ref = rest[-1]
    a = a_ref[...]
    rs = _row_scale(ss_ref, width)
    g = _dot(a, wg_ref[...]) * rs
    u = _dot(a, wu_ref[...]) * rs
    o_ref[...] = (g / (1.0 + jnp.exp(-g)) * u).astype(o_ref.dtype)


def _swiglu(a, ss, wg, wu, layer, col0, ncols, tm, tn, into=None):
    m, k = a.shape
    n = wg.shape[2]
    assert col0 % tn == 0 and ncols % tn == 0
    blk0 = col0 // tn
    in_specs = [pl.BlockSpec((tm, k), lambda i, j: (i, 0)),
                pl.BlockSpec((tm, 1), lambda i, j: (i, 0)),
                pl.BlockSpec((None, k, tn), lambda i, j: (layer, 0, blk0 + j)),
                pl.BlockSpec((None, k, tn), lambda i, j: (layer, 0, blk0 + j))]
    args = [a, ss, wg, wu]
    aliases = {}
    if into is not None:
        in_specs.append(pl.BlockSpec(memory_space=pl.ANY))
        aliases = {len(args): 0}
        args.append(into)
    return pl.pallas_call(
        functools.partial(_swiglu_body, width=k),
        grid=(m // tm, ncols // tn),
        in_specs=in_specs,
        out_specs=pl.BlockSpec((tm, tn), lambda i, j: (i, blk0 + j)),
        out_shape=jax.ShapeDtypeStruct((m, n), BF16),
        input_output_aliases=aliases,
        compiler_params=_params("parallel", "arbitrary"),
        name="ffn_gate_up",
    )(*args)


def _down_body(a_ref, w_ref, r_ref, o_ref):
    o_ref[...] = r_ref[...] + _dot(a_ref[...], w_ref[...])


def _down_norm_body(a_ref, w_ref, r_ref, g_ref, o_ref, xg_ref, ss_ref):
    x = r_ref[...] + _dot(a_ref[...], w_ref[...])
    o_ref[...] = x
    _emit_norm_inputs(x, g_ref, xg_ref, ss_ref, pl.program_id(1) == 0)


def _down_residual(a, w, layer, part, parts, res, g, tm, tn):
    m = a.shape[0]
    n = w.shape[2]
    k = w.shape[1] // parts
    assert k * parts == w.shape[1] and k % LANES == 0
    in_specs = [pl.BlockSpec((tm, k), lambda i, j: (i, part)),
                pl.BlockSpec((None, k, tn), lambda i, j: (layer, part, j)),
                pl.BlockSpec((tm, tn), lambda i, j: (i, j))]
    out_specs = [pl.BlockSpec((tm, tn), lambda i, j: (i, j))]
    out_shape = [jax.ShapeDtypeStruct((m, n), F32)]
    args = [a, w, res]
    body = _down_body
    if g is not None:
        nspecs, nshapes = _norm_out(m, n, tm, tn, lambda i, j: (i, j))
        in_specs.append(pl.BlockSpec((1, tn), lambda i, j: (0, j)))
        out_specs += nspecs
        out_shape += nshapes
        args.append(g)
        body = _down_norm_body
    return pl.pallas_call(
        body,
        grid=(m // tm, n // tn),
        in_specs=in_specs,
        out_specs=out_specs,
        out_shape=out_shape,
        compiler_params=_params("parallel", "arbitrary"),
        name="ffn_down",
    )(*args)


def _gla_body(q_ref, k_ref, v_ref, r_ref, lr_ref, wup_ref, bg_ref, gn_ref, o_ref, st_ref,
              *, n_chunks, heads, dk, dv):
    @pl.when(pl.program_id(1) == 0)
    def _():
        st_ref[...] = jnp.zeros_like(st_ref)

    c = GLA_CHUNK
    row = lax.broadcasted_iota(jnp.int32, (c, c), 0)
    col = lax.broadcasted_iota(jnp.int32, (c, c), 1)
    causal = col <= row
    tri = causal.astype(BF16)
    tri2 = jnp.concatenate([tri, tri], axis=1)
    scale = dk ** -0.5

    for ci in range(n_chunks):
        rows = pl.ds(ci * c, c)
        logits = _dot(lr_ref[rows, :].astype(BF16), wup_ref[...]) + bg_ref[...]
        log_alpha = -_softplus(-logits) / GLA_GATE_TEMP
        b_all = _dot(tri2, jnp.concatenate(_split_bf16(log_alpha), axis=0))
        for h in range(heads):
            ks = pl.ds(h * dk, dk)
            vs = pl.ds(h * dv, dv)
            b = b_all[:, h * dk:(h + 1) * dk]
            b_last = b[c - 1:c, :]
            q = q_ref[rows, ks].astype(F32)
            k = k_ref[rows, ks].astype(F32)
            v = v_ref[rows, vs]
            q_dec = (q * scale * jnp.exp(b)).astype(BF16)
            k_intra = (k * jnp.exp(-b)).astype(BF16)
            k_state = (k * jnp.exp(b_last - b)).astype(BF16)

            scores = jnp.where(causal, _dot_nt(q_dec, k_intra), 0.0)
            st = st_ref[h]
            o = _dot(scores.astype(BF16), v) + _dot_nt(q_dec, st.astype(BF16))
            st_ref[h] = st * jnp.exp(b_last) + _dot_tn(v, k_state)

            r = r_ref[rows, vs].astype(F32)
            o = _rms_rows(o, gn_ref[...]) * (r / (1.0 + jnp.exp(-r)))
            o_ref[rows, vs] = o.astype(o_ref.dtype)


def _gla(proj, lr, wup, bg, gn, *, heads, dk, dv, rows):
    bsz, t, _ = proj.shape
    kw = heads * dk
    vw = heads * dv
    assert (2 * kw) % vw == 0
    v_blk = (2 * kw) // vw
    body = functools.partial(_gla_body, n_chunks=rows // GLA_CHUNK, heads=heads, dk=dk, dv=dv)
    return pl.pallas_call(
        body,
        grid=(bsz, t // rows),
        in_specs=[
            pl.BlockSpec((None, rows, kw), lambda b, i: (b, i, 0)),
            pl.BlockSpec((None, rows, kw), lambda b, i: (b, i, 1)),
            pl.BlockSpec((None, rows, vw), lambda b, i: (b, i, v_blk)),
            pl.BlockSpec((None, rows, vw), lambda b, i: (b, i, v_blk + 1)),
            pl.BlockSpec((None, rows, LANES), lambda b, i: (b, i, 0)),
            pl.BlockSpec((LANES, kw), lambda b, i: (0, 0)),
            pl.BlockSpec((1, kw), lambda b, i: (0, 0)),
            pl.BlockSpec((1, dv), lambda b, i: (0, 0)),
        ],
        out_specs=pl.BlockSpec((None, rows, vw), lambda b, i: (b, i, 0)),
        out_shape=jax.ShapeDtypeStruct((bsz, t, vw), BF16),
        scratch_shapes=[pltpu.VMEM((heads, dv, dk), F32)],
        compiler_params=_params("parallel", "arbitrary"),
        name="gla",
    )(proj, proj, proj, proj, lr, wup, bg, gn)


def _sb_body(q_ref, k_ref, v_ref, gq_ref, gk_ref, o_ref, kn_ref, qn_ref, acc_ref, carry_ref,
             lt_ref, *, tq, d, group):
    qi = pl.program_id(2)
    heads = [pl.ds(g * d, d) for g in range(group)]
    sub = tq // d

    @pl.when(qi == 0)
    def _():
        for hs in heads:
            kn_ref[:, hs] = _rms_rows(k_ref[:, hs].astype(F32), gk_ref[...]).astype(kn_ref.dtype)
        j = lax.broadcasted_iota(jnp.int32, (2 * d, 2 * d), 0) % d
        s = lax.broadcasted_iota(jnp.int32, (2 * d, 2 * d), 1)
        lt_ref[...] = ((j > s) | (s >= d)).astype(lt_ref.dtype)

    scale = d ** -0.5
    for hs in heads:
        qn_ref[:, hs] = (_rms_rows(q_ref[:, hs].astype(F32), gq_ref[...]) * scale).astype(qn_ref.dtype)
    acc_ref[...] = jnp.zeros_like(acc_ref)
    carry_ref[...] = jnp.zeros_like(carry_ref)

    def key_block(base, masks):
        tiles = [(hs, s) for hs in heads for s in reversed(range(sub))]
        blk = {s: pl.ds(pl.multiple_of(base + s * d, d), d) for s in range(sub)}
        z = [_dot_nt(qn_ref[:, hs], kn_ref[blk[s], hs]) for hs, s in tiles]
        sp = [_softplus(zt) for zt in z]
        if masks is None:
            keep = sp
        else:
            keep = [jnp.where(masks[s], spt, 0.0) for spt, (_, s) in zip(sp, tiles)]
        sums = [_dot(jnp.concatenate(_split_bf16(kt), axis=1), lt_ref[...]) for kt in keep]
        low = None
        it = iter(range(len(tiles)))
        for hs in heads:
            carry = carry_ref[:, hs]
            out = None
            for s in reversed(range(sub)):
                t = next(it)
                w = jnp.exp(z[t] - sp[t] - sums[t][:, :d] - carry)
                if masks is not None:
                    w = jnp.where(masks[s], w, 0.0)
                part = _dot(w.astype(BF16), v_ref[blk[s], hs])
                out = part if out is None else out + part
                carry = carry + sums[t][:, d:]
            carry_ref[:, hs] = carry
            acc_ref[:, hs] += out
            low = jnp.min(carry) if low is None else jnp.minimum(low, jnp.min(carry))
        return low

    row = lax.broadcasted_iota(jnp.int32, (tq, d), 0)
    col = lax.broadcasted_iota(jnp.int32, (tq, d), 1)
    low = key_block(qi * tq, [col + s * d < row for s in range(sub)])

    def more(state):
        n, low = state
        return jnp.logical_and(n < qi, low < SB_ZERO_WEIGHT_CARRY)

    def step(state):
        n, _ = state
        return n + 1, key_block((qi - 1 - n) * tq, None)

    lax.while_loop(more, step, (jnp.int32(0), low))
    o_ref[...] = acc_ref[...].astype(o_ref.dtype)


def _stick_breaking(proj, gq, gk, *, heads, d, col0, tq, group):
    bsz, t, _ = proj.shape
    assert d == LANES and tq % d == 0
    gw = group * d
    q0 = col0 // gw
    k0 = q0 + heads // group
    v0 = k0 + heads // group
    body = functools.partial(_sb_body, tq=tq, d=d, group=group)
    return pl.pallas_call(
        body,
        grid=(bsz, heads // group, t // tq),
        in_specs=[
            pl.BlockSpec((None, tq, gw), lambda b, h, i: (b, i, q0 + h)),
            pl.BlockSpec((None, t, gw), lambda b, h, i: (b, 0, k0 + h)),
            pl.BlockSpec((None, t, gw), lambda b, h, i: (b, 0, v0 + h)),
            pl.BlockSpec((1, d), lambda b, h, i: (0, 0)),
            pl.BlockSpec((1, d), lambda b, h, i: (0, 0)),
        ],
        out_specs=pl.BlockSpec((None, tq, gw), lambda b, h, i: (b, i, h)),
        out_shape=jax.ShapeDtypeStruct((bsz, t, heads * d), BF16),
        scratch_shapes=[pltpu.VMEM((t, gw), BF16),
                        pltpu.VMEM((tq, gw), BF16),
                        pltpu.VMEM((tq, gw), F32),
                        pltpu.VMEM((tq, gw), F32),
                        pltpu.VMEM((2 * d, 2 * d), BF16)],
        compiler_params=_params("parallel", "parallel", "arbitrary"),
        name="stick_breaking",
    )(proj, proj, proj, gq, gk)


def _pad_to(a, axis, size):
    pad = [(0, 0)] * a.ndim
    pad[axis] = (0, size - a.shape[axis])
    return jnp.pad(a, pad)


def kernel(x, norm_mix, w_in, w_gate_up, b_gate, gla_o_norm, sb_q_norm, sb_k_norm, w_out,
           norm_ffn, w_ffn_gate, w_ffn_up, w_ffn_down):
    bsz, t, d_model = x.shape
    depth = w_in.shape[0]
    n = bsz * t
    gla_kw = w_gate_up.shape[2]
    gla_dv = gla_o_norm.shape[1]
    gla_w = GLA_HEADS * gla_dv
    gla_dk = gla_kw // GLA_HEADS
    lr0 = 2 * gla_kw + 2 * gla_w
    sb_w = (w_in.shape[2] - lr0 - GLA_GATE_RANK) // 3
    sb_heads = sb_w // SB_HEAD_DIM
    f_hidden = w_ffn_gate.shape[2]
    tn_ffn = min(512, f_hidden)
    f_main = f_hidden // tn_ffn * tn_ffn

    out_dtype = x.dtype
    x = x.astype(F32).reshape(n, d_model)

    w_in_bf = w_in.astype(BF16)
    w_sb = w_in_bf[:, :, lr0 + GLA_GATE_RANK:]
    w_up = _pad_to(w_gate_up, 1, LANES).astype(BF16)
    w_o = w_out.astype(BF16)
    w_g = w_ffn_gate.astype(BF16)
    w_u = w_ffn_up.astype(BF16)
    w_d = w_ffn_down.astype(BF16)
    g_mix = norm_mix.reshape(depth, 1, d_model)
    g_ffn = norm_ffn.reshape(depth, 1, d_model)

    tm = min(1024, n)
    xg, ss = _prenorm(x, g_mix[0], min(512, n))
    for l in range(depth):
        proj_gla = _matmul_scaled(xg, ss, w_in_bf, l, 0, lr0, tm, min(1024, lr0), BF16, "in_proj_gla")
        lr = _matmul_scaled(xg, ss, w_in_bf, l, lr0, LANES, tm, LANES, F32, "in_proj_gate")
        proj_sb = _matmul_scaled(xg, ss, w_sb, l, 0, 3 * sb_w, tm, min(1024, 3 * sb_w), BF16, "in_proj_sb")
        o_a = _gla(proj_gla.reshape(bsz, t, lr0), lr.reshape(bsz, t, LANES), w_up[l],
                   b_gate[l].reshape(1, -1), gla_o_norm[l].reshape(1, -1),
                   heads=GLA_HEADS, dk=gla_dk, dv=gla_dv, rows=min(256, t))
        o_b = _stick_breaking(proj_sb.reshape(bsz, t, 3 * sb_w), sb_q_norm[l].reshape(1, -1),
                              sb_k_norm[l].reshape(1, -1), heads=sb_heads, d=SB_HEAD_DIM, col0=0,
                              tq=min(256, t), group=4)
        x, xg, ss = _matmul2_residual(o_a.reshape(n, gla_w), o_b.reshape(n, sb_w), w_o, l, x,
                                      g_ffn[l], tm, min(512, d_model))

        u = _swiglu(xg, ss, w_g, w_u, l, 0, f_main, tm, tn_ffn)
        if f_main < f_hidden:
            u = _swiglu(xg, ss, w_g, w_u, l, f_main, f_hidden - f_main, tm, f_hidden - f_main, into=u)
        (x,) = _down_residual(u, w_d, l, 0, 2, x, None, tm, min(512, d_model))
        if l + 1 < depth:
            x, xg, ss = _down_residual(u, w_d, l, 1, 2, x, g_mix[l + 1], tm, min(512, d_model))
        else:
            (x,) = _down_residual(u, w_d, l, 1, 2, x, None, tm, min(512, d_model))
    return x.reshape(bsz, t, d_model).astype(out_dtype)
```

```python
import functools

import jax
import jax.numpy as jnp
from jax import lax
from jax.experimental import pallas as pl
from jax.experimental.pallas import tpu as pltpu

F32 = jnp.float32
BF16 = jnp.bfloat16

LANES = 128
V7X_VMEM_LIMIT_BYTES = 56 * 1024 * 1024

NORM_EPS = 1e-6
GLA_HEADS = 4
GLA_GATE_RANK = 16
GLA_GATE_TEMP = 16.0
GLA_CHUNK = 64
SB_HEAD_DIM = 128
SB_ZERO_WEIGHT_CARRY = 105.0


def _params(*sem):
    return pltpu.CompilerParams(dimension_semantics=sem,
                                vmem_limit_bytes=V7X_VMEM_LIMIT_BYTES)


def _dot(a, b):
    return jnp.dot(a, b, preferred_element_type=F32)


def _dot_nt(a, b):
    return lax.dot_general(a, b, (((1,), (1,)), ((), ())), preferred_element_type=F32)


def _dot_tn(a, b):
    return lax.dot_general(a, b, (((0,), (0,)), ((), ())), preferred_element_type=F32)


def _split_bf16(a):
    hi = a.astype(BF16)
    lo = (a - hi.astype(F32)).astype(BF16)
    return hi, lo


def _softplus(z):
    return jnp.maximum(z, 0.0) + jnp.log(1.0 + jnp.exp(-jnp.abs(z)))


def _rms_rows(x, g):
    return x * lax.rsqrt(jnp.mean(x * x, axis=-1, keepdims=True) + NORM_EPS) * g


def _row_scale(ss_ref, width):
    return lax.rsqrt(ss_ref[...] * (1.0 / width) + NORM_EPS)


def _emit_norm_inputs(x, g_ref, xg_ref, ss_ref, first):
    xg_ref[...] = (x * g_ref[...]).astype(xg_ref.dtype)
    part = jnp.sum(x * x, axis=-1, keepdims=True)

    @pl.when(first)
    def _():
        ss_ref[...] = part

    @pl.when(jnp.logical_not(first))
    def _():
        ss_ref[...] += part


def _norm_out(m, n, tm, tn, idx):
    specs = [pl.BlockSpec((tm, tn), idx), pl.BlockSpec((tm, 1), lambda i, *_: (i, 0))]
    shapes = [jax.ShapeDtypeStruct((m, n), BF16), jax.ShapeDtypeStruct((m, 1), F32)]
    return specs, shapes


def _prenorm_body(x_ref, g_ref, xg_ref, ss_ref):
    x = x_ref[...]
    xg_ref[...] = (x * g_ref[...]).astype(xg_ref.dtype)
    ss_ref[...] = jnp.sum(x * x, axis=-1, keepdims=True)


def _prenorm(x, g, tm):
    n, d = x.shape
    specs, shapes = _norm_out(n, d, tm, d, lambda i: (i, 0))
    return pl.pallas_call(
        _prenorm_body,
        grid=(n // tm,),
        in_specs=[pl.BlockSpec((tm, d), lambda i: (i, 0)),
                  pl.BlockSpec((1, d), lambda i: (0, 0))],
        out_specs=specs,
        out_shape=shapes,
        compiler_params=_params("parallel"),
        name="prenorm",
    )(x, g)


def _cast_specs(src, layer, rows, cols, row_blocks, col_blocks, idx):
    br, bc = rows // row_blocks, cols // col_blocks
    assert br * row_blocks == rows and bc * col_blocks == cols and br % 16 == 0 and bc % LANES == 0
    return (pl.BlockSpec((None, br, bc), lambda *g: (layer,) + idx(*g)),
            pl.BlockSpec((None, br, bc), lambda *g: (0,) + idx(*g)),
            jax.ShapeDtypeStruct((1, rows, cols), BF16))


def _mm_body(a_ref, ss_ref, w_ref, src_ref, o_ref, dst_ref, *, width):
    o_ref[...] = (_dot(a_ref[...], w_ref[...]) * _row_scale(ss_ref, width)).astype(o_ref.dtype)
    dst_ref[...] = src_ref[...].astype(dst_ref.dtype)


def _matmul_scaled(a, ss, w, layer, ncols, tm, tn, cast_src, cast_cols, name):
    m, k = a.shape
    assert ncols % tn == 0
    grid = (m // tm, ncols // tn)
    c_in, c_out, c_shape = _cast_specs(cast_src, layer, cast_src.shape[1], cast_cols, *grid,
                                       lambda i, j: (i, j))
    return pl.pallas_call(
        functools.partial(_mm_body, width=k),
        grid=grid,
        in_specs=[pl.BlockSpec((tm, k), lambda i, j: (i, 0)),
                  pl.BlockSpec((tm, 1), lambda i, j: (i, 0)),
                  pl.BlockSpec((None, k, tn), lambda i, j: (layer, 0, j)),
                  c_in],
        out_specs=[pl.BlockSpec((tm, tn), lambda i, j: (i, j)), c_out],
        out_shape=[jax.ShapeDtypeStruct((m, ncols), BF16), c_shape],
        compiler_params=_params("parallel", "arbitrary"),
        name=name,
    )(a, ss, w, cast_src)


def _mm_gate_body(a_ref, ss_ref, w_ref, wgate_ref, src_ref, o_ref, gate_ref, dst_ref, *, width):
    a = a_ref[...]
    rs = _row_scale(ss_ref, width)
    o_ref[...] = (_dot(a, w_ref[...]) * rs).astype(o_ref.dtype)
    dst_ref[...] = src_ref[...].astype(dst_ref.dtype)

    @pl.when(pl.program_id(1) == 0)
    def _():
        gate_ref[...] = _dot(a, wgate_ref[...]) * rs


def _matmul_scaled_with_gate(a, ss, w, layer, ncols, gate_col0, tm, tn, cast_src, cast_cols):
    m, k = a.shape
    assert ncols % tn == 0 and gate_col0 % LANES == 0
    gate_blk = gate_col0 // LANES
    grid = (m // tm, ncols // tn)
    c_in, c_out, c_shape = _cast_specs(cast_src, layer, cast_src.shape[1], cast_cols, *grid,
                                       lambda i, j: (i, j))
    return pl.pallas_call(
        functools.partial(_mm_gate_body, width=k),
        grid=grid,
        in_specs=[pl.BlockSpec((tm, k), lambda i, j: (i, 0)),
                  pl.BlockSpec((tm, 1), lambda i, j: (i, 0)),
                  pl.BlockSpec((None, k, tn), lambda i, j: (layer, 0, j)),
                  pl.BlockSpec((None, k, LANES), lambda i, j: (layer, 0, gate_blk)),
                  c_in],
        out_specs=[pl.BlockSpec((tm, tn), lambda i, j: (i, j)),
                   pl.BlockSpec((tm, LANES), lambda i, j: (i, 0)),
                   c_out],
        out_shape=[jax.ShapeDtypeStruct((m, ncols), BF16),
                   jax.ShapeDtypeStruct((m, LANES), F32),
                   c_shape],
        compiler_params=_params("parallel", "arbitrary"),
        name="in_proj_gla",
    )(a, ss, w, w, cast_src)


def _mm2_res_body(a1_ref, a2_ref, w1_ref, w2_ref, r_ref, g_ref, o_ref, xg_ref, ss_ref):
    x = r_ref[...] + (_dot(a1_ref[...], w1_ref[...]) + _dot(a2_ref[...], w2_ref[...]))
    o_ref[...] = x
    _emit_norm_inputs(x, g_ref, xg_ref, ss_ref, pl.program_id(1) == 0)


def _matmul2_residual(a1, a2, w, layer, res, g, tm, tn):
    m, kh = a1.shape
    assert a2.shape == a1.shape and w.shape[1] == 2 * kh
    n = w.shape[2]
    nspecs, nshapes = _norm_out(m, n, tm, tn, lambda i, j: (i, j))
    return pl.pallas_call(
        _mm2_res_body,
        grid=(m // tm, n // tn),
        in_specs=[pl.BlockSpec((tm, kh), lambda i, j: (i, 0)),
                  pl.BlockSpec((tm, kh), lambda i, j: (i, 0)),
                  pl.BlockSpec((None, kh, tn), lambda i, j: (layer, 0, j)),
                  pl.BlockSpec((None, kh, tn), lambda i, j: (layer, 1, j)),
                  pl.BlockSpec((tm, tn), lambda i, j: (i, j)),
                  pl.BlockSpec((1, tn), lambda i, j: (0, j))],
        out_specs=[pl.BlockSpec((tm, tn), lambda i, j: (i, j))] + nspecs,
        out_shape=[jax.ShapeDtypeStruct((m, n), F32)] + nshapes,
        compiler_params=_params("parallel", "arbitrary"),
        name="out_proj",
    )(a1, a2, w, w, res, g)


def _swiglu_body(a_ref, ss_ref, wg_ref, wu_ref, *rest, width):
    o_ref = rest[-1]
    a = a_ref[...]
    rs = _row_scale(ss_ref, width)
    g = _dot(a, wg_ref[...]) * rs
    u = _dot(a, wu_ref[...]) * rs
    o_ref[...] = (g / (1.0 + jnp.exp(-g)) * u).astype(o_ref.dtype)


def _swiglu(a, ss, wg, wu, layer, w_col0, out_col0, ncols, n_out, tm, tn, into=None):
    m, k = a.shape
    assert w_col0 % tn == 0 and out_col0 % tn == 0 and ncols % tn == 0
    w_blk0, o_blk0 = w_col0 // tn, out_col0 // tn
    in_specs = [pl.BlockSpec((tm, k), lambda i, j: (i, 0)),
                pl.BlockSpec((tm, 1), lambda i, j: (i, 0)),
                pl.BlockSpec((None, k, tn), lambda i, j: (layer, 0, w_blk0 + j)),
                pl.BlockSpec((None, k, tn), lambda i, j: (layer, 0, w_blk0 + j))]
    args = [a, ss, wg, wu]
    aliases = {}
    if into is not None:
        in_specs.append(pl.BlockSpec(memory_space=pl.ANY))
        aliases = {len(args): 0}
        args.append(into)
    return pl.pallas_call(
        functools.partial(_swiglu_body, width=k),
        grid=(m // tm, ncols // tn),
        in_specs=in_specs,
        out_specs=pl.BlockSpec((tm, tn), lambda i, j: (i, o_blk0 + j)),
        out_shape=jax.ShapeDtypeStruct((m, n_out), BF16),
        input_output_aliases=aliases,
        compiler_params=_params("parallel", "arbitrary"),
        name="ffn_gate_up",
    )(*args)


def _down_body(a_ref, w_ref, r_ref, o_ref):
    o_ref[...] = r_ref[...] + _dot(a_ref[...], w_ref[...])


def _down_norm_body(a_ref, w_ref, r_ref, g_ref, o_ref, xg_ref, ss_ref):
    x = r_ref[...] + _dot(a_ref[...], w_ref[...])
    o_ref[...] = x
    _emit_norm_inputs(x, g_ref, xg_ref, ss_ref, pl.program_id(1) == 0)


def _down_residual(a, w, layer, part, parts, res, g, tm, tn):
    m = a.shape[0]
    n = w.shape[2]
    k = w.shape[1] // parts
    assert k * parts == w.shape[1] and k % LANES == 0
    in_specs = [pl.BlockSpec((tm, k), lambda i, j: (i, part)),
                pl.BlockSpec((None, k, tn), lambda i, j: (layer, part, j)),
                pl.BlockSpec((tm, tn), lambda i, j: (i, j))]
    out_specs = [pl.BlockSpec((tm, tn), lambda i, j: (i, j))]
    out_shape = [jax.ShapeDtypeStruct((m, n), F32)]
    args = [a, w, res]
    body = _down_body
    if g is not None:
        nspecs, nshapes = _norm_out(m, n, tm, tn, lambda i, j: (i, j))
        in_specs.append(pl.BlockSpec((1, tn), lambda i, j: (0, j)))
        out_specs += nspecs
        out_shape += nshapes
        args.append(g)
        body = _down_norm_body
    return pl.pallas_call(
        body,
        grid=(m // tm, n // tn),
        in_specs=in_specs,
        out_specs=out_specs,
        out_shape=out_shape,
        compiler_params=_params("parallel", "arbitrary"),
        name="ffn_down",
    )(*args)


def _gla_body(q_ref, k_ref, v_ref, r_ref, lr_ref, wup_ref, bg_ref, gn_ref, o_ref, st_ref,
              *, n_chunks, heads, dk, dv):
    @pl.when(pl.program_id(1) == 0)
    def _():
        st_ref[...] = jnp.zeros_like(st_ref)

    c = GLA_CHUNK
    row = lax.broadcasted_iota(jnp.int32, (c, c), 0)
    col = lax.broadcasted_iota(jnp.int32, (c, c), 1)
    causal = col <= row
    tri = causal.astype(BF16)
    tri2 = jnp.concatenate([tri, tri], axis=1)
    scale = dk ** -0.5

    for ci in range(n_chunks):
        rows = pl.ds(ci * c, c)
        logits = _dot(lr_ref[rows, :].astype(BF16), wup_ref[...]) + bg_ref[...]
        log_alpha = -_softplus(-logits) / GLA_GATE_TEMP
        b_all = _dot(tri2, jnp.concatenate(_split_bf16(log_alpha), axis=0))
        for h in range(heads):
            ks = pl.ds(h * dk, dk)
            vs = pl.ds(h * dv, dv)
            b = b_all[:, h * dk:(h + 1) * dk]
            b_last = b[c - 1:c, :]
            q = q_ref[rows, ks].astype(F32)
            k = k_ref[rows, ks].astype(F32)
            v = v_ref[rows, vs]
            q_dec = (q * scale * jnp.exp(b)).astype(BF16)
            k_intra = (k * jnp.exp(-b)).astype(BF16)
            k_state = (k * jnp.exp(b_last - b)).astype(BF16)

            scores = jnp.where(causal, _dot_nt(q_dec, k_intra), 0.0)
            st = st_ref[h]
            o = _dot(scores.astype(BF16), v) + _dot_nt(q_dec, st.astype(BF16))
            st_ref[h] = st * jnp.exp(b_last) + _dot_tn(v, k_state)

            r = r_ref[rows, vs].astype(F32)
            o = _rms_rows(o, gn_ref[...]) * (r / (1.0 + jnp.exp(-r)))
            o_ref[rows, vs] = o.astype(o_ref.dtype)


def _gla(proj, lr, wup, bg, gn, *, heads, dk, dv, rows):
    bsz, t, _ = proj.shape
    kw = heads * dk
    vw = heads * dv
    assert (2 * kw) % vw == 0
    v_blk = (2 * kw) // vw
    body = functools.partial(_gla_body, n_chunks=rows // GLA_CHUNK, heads=heads, dk=dk, dv=dv)
    return pl.pallas_call(
        body,
        grid=(bsz, t // rows),
        in_specs=[
            pl.BlockSpec((None, rows, kw), lambda b, i: (b, i, 0)),
            pl.BlockSpec((None, rows, kw), lambda b, i: (b, i, 1)),
            pl.BlockSpec((None, rows, vw), lambda b, i: (b, i, v_blk)),
            pl.BlockSpec((None, rows, vw), lambda b, i: (b, i, v_blk + 1)),
            pl.BlockSpec((None, rows, LANES), lambda b, i: (b, i, 0)),
            pl.BlockSpec((LANES, kw), lambda b, i: (0, 0)),
            pl.BlockSpec((1, kw), lambda b, i: (0, 0)),
            pl.BlockSpec((1, dv), lambda b, i: (0, 0)),
        ],
        out_specs=pl.BlockSpec((None, rows, vw), lambda b, i: (b, i, 0)),
        out_shape=jax.ShapeDtypeStruct((bsz, t, vw), BF16),
        scratch_shapes=[pltpu.VMEM((heads, dv, dk), F32)],
        compiler_params=_params("parallel", "arbitrary"),
        name="gla",
    )(proj, proj, proj, proj, lr, wup, bg, gn)


def _sb_body(q_ref, k_ref, v_ref, gq_ref, gk_ref, src_ref, o_ref, dst_ref, kn_ref, qn_ref, acc_ref,
             carry_ref, lt_ref, *, tq, d, group):
    qi = pl.program_id(2)
    heads = [pl.ds(g * d, d) for g in range(group)]
    sub = tq // d
    dst_ref[...] = src_ref[...].astype(dst_ref.dtype)

    @pl.when(qi == 0)
    def _():
        for hs in heads:
            kn_ref[:, hs] = _rms_rows(k_ref[:, hs].astype(F32), gk_ref[...]).astype(kn_ref.dtype)
        j = lax.broadcasted_iota(jnp.int32, (2 * d, 2 * d), 0) % d
        s = lax.broadcasted_iota(jnp.int32, (2 * d, 2 * d), 1)
        lt_ref[...] = ((j > s) | (s >= d)).astype(lt_ref.dtype)

    scale = d ** -0.5
    for hs in heads:
        qn_ref[:, hs] = (_rms_rows(q_ref[:, hs].astype(F32), gq_ref[...]) * scale).astype(qn_ref.dtype)
    acc_ref[...] = jnp.zeros_like(acc_ref)
    carry_ref[...] = jnp.zeros_like(carry_ref)

    def key_block(base, masks):
        tiles = [(hs, s) for hs in heads for s in reversed(range(sub))]
        blk = {s: pl.ds(pl.multiple_of(base + s * d, d), d) for s in range(sub)}
        z = [_dot_nt(qn_ref[:, hs], kn_ref[blk[s], hs]) for hs, s in tiles]
        sp = [_softplus(zt) for zt in z]
        if masks is None:
            keep = sp
        else:
            keep = [jnp.where(masks[s], spt, 0.0) for spt, (_, s) in zip(sp, tiles)]
        sums = [_dot(jnp.concatenate(_split_bf16(kt), axis=1), lt_ref[...]) for kt in keep]
        low = None
        it = iter(range(len(tiles)))
        for hs in heads:
            carry = carry_ref[:, hs]
            out = None
            for s in reversed(range(sub)):
                t = next(it)
                w = jnp.exp(z[t] - sp[t] - sums[t][:, :d] - carry)
                if masks is not None:
                    w = jnp.where(masks[s], w, 0.0)
                part = _dot(w.astype(BF16), v_ref[blk[s], hs])
                out = part if out is None else out + part
                carry = carry + sums[t][:, d:]
            carry_ref[:, hs] = carry
            acc_ref[:, hs] += out
            low = jnp.min(carry) if low is None else jnp.minimum(low, jnp.min(carry))
        return low

    row = lax.broadcasted_iota(jnp.int32, (tq, d), 0)
    col = lax.broadcasted_iota(jnp.int32, (tq, d), 1)
    low = key_block(qi * tq, [col + s * d < row for s in range(sub)])

    def more(state):
        n, low = state
        return jnp.logical_and(n < qi, low < SB_ZERO_WEIGHT_CARRY)

    def step(state):
        n, _ = state
        return n + 1, key_block((qi - 1 - n) * tq, None)

    lax.while_loop(more, step, (jnp.int32(0), low))
    o_ref[...] = acc_ref[...].astype(o_ref.dtype)


def _stick_breaking(proj, gq, gk, cast_src, layer, *, heads, d, col0, tq, group):
    bsz, t, _ = proj.shape
    assert d == LANES and tq % d == 0
    gw = group * d
    q0 = col0 // gw
    k0 = q0 + heads // group
    v0 = k0 + heads // group
    nq = t // tq
    body = functools.partial(_sb_body, tq=tq, d=d, group=group)
    c_in, c_out, c_shape = _cast_specs(cast_src, layer, cast_src.shape[1], cast_src.shape[2],
                                       bsz, heads // group * nq, lambda b, h, i: (b, h * nq + i))
    return pl.pallas_call(
        body,
        grid=(bsz, heads // group, nq),
        in_specs=[
            pl.BlockSpec((None, tq, gw), lambda b, h, i: (b, i, q0 + h)),
            pl.BlockSpec((None, t, gw), lambda b, h, i: (b, 0, k0 + h)),
            pl.BlockSpec((None, t, gw), lambda b, h, i: (b, 0, v0 + h)),
            pl.BlockSpec((1, d), lambda b, h, i: (0, 0)),
            pl.BlockSpec((1, d), lambda b, h, i: (0, 0)),
            c_in,
        ],
        out_specs=[pl.BlockSpec((None, tq, gw), lambda b, h, i: (b, i, h)), c_out],
        out_shape=[jax.ShapeDtypeStruct((bsz, t, heads * d), BF16), c_shape],
        scratch_shapes=[pltpu.VMEM((t, gw), BF16),
                        pltpu.VMEM((tq, gw), BF16),
                        pltpu.VMEM((tq, gw), F32),
                        pltpu.VMEM((tq, gw), F32),
                        pltpu.VMEM((2 * d, 2 * d), BF16)],
        compiler_params=_params("parallel", "parallel", "arbitrary"),
        name="stick_breaking",
    )(proj, proj, proj, gq, gk, cast_src)


def _pad_to(a, axis, size):
    pad = [(0, 0)] * a.ndim
    pad[axis] = (0, size - a.shape[axis])
    return jnp.pad(a, pad)


def kernel(x, norm_mix, w_in, w_gate_up, b_gate, gla_o_norm, sb_q_norm, sb_k_norm, w_out,
           norm_ffn, w_ffn_gate, w_ffn_up, w_ffn_down):
    bsz, t, d_model = x.shape
    depth = w_in.shape[0]
    n = bsz * t
    gla_kw = w_gate_up.shape[2]
    gla_dv = gla_o_norm.shape[1]
    gla_w = GLA_HEADS * gla_dv
    gla_dk = gla_kw // GLA_HEADS
    lr0 = 2 * gla_kw + 2 * gla_w
    sb_w = (w_in.shape[2] - lr0 - GLA_GATE_RANK) // 3
    sb_heads = sb_w // SB_HEAD_DIM
    f_hidden = w_ffn_gate.shape[2]
    tn_ffn = min(512, f_hidden)
    f_main = f_hidden // tn_ffn * tn_ffn

    out_dtype = x.dtype
    x = x.astype(F32).reshape(n, d_model)

    w_in_bf = w_in[:, :, :lr0 + LANES].astype(BF16)
    w_sb = w_in[:, :, lr0 + GLA_GATE_RANK:].astype(BF16)
    w_up = _pad_to(w_gate_up, 1, LANES).astype(BF16)
    w_o = w_out.astype(BF16)
    w_g_tail = w_ffn_gate[:, :, f_main:].astype(BF16)
    w_u_tail = w_ffn_up[:, :, f_main:].astype(BF16)
    g_mix = norm_mix.reshape(depth, 1, d_model)
    g_ffn = norm_ffn.reshape(depth, 1, d_model)

    tm = min(1024, n)
    xg, ss = _prenorm(x, g_mix[0], min(512, n))
    for l in range(depth):
        proj_gla, lr, w_g = _matmul_scaled_with_gate(xg, ss, w_in_bf, l, lr0, lr0, tm, min(1024, lr0),
                                                     w_ffn_gate, f_main)
        proj_sb, w_u = _matmul_scaled(xg, ss, w_sb, l, 3 * sb_w, tm, min(1024, 3 * sb_w),
                                      w_ffn_up, f_main, "in_proj_sb")
        o_a = _gla(proj_gla.reshape(bsz, t, lr0), lr.reshape(bsz, t, LANES), w_up[l],
                   b_gate[l].reshape(1, -1), gla_o_norm[l].reshape(1, -1),
                   heads=GLA_HEADS, dk=gla_dk, dv=gla_dv, rows=min(256, t))
        o_b, w_d = _stick_breaking(proj_sb.reshape(bsz, t, 3 * sb_w), sb_q_norm[l].reshape(1, -1),
                                   sb_k_norm[l].reshape(1, -1), w_ffn_down, l, heads=sb_heads,
                                   d=SB_HEAD_DIM, col0=0, tq=min(256, t), group=4)
        x, xg, ss = _matmul2_residual(o_a.reshape(n, gla_w), o_b.reshape(n, sb_w), w_o, l, x,
                                      g_ffn[l], tm, min(512, d_model))

        u = _swiglu(xg, ss, w_g, w_u, 0, 0, 0, f_main, f_hidden, tm, tn_ffn)
        if f_main < f_hidden:
            f_tail = f_hidden - f_main
            u = _swiglu(xg, ss, w_g_tail, w_u_tail, l, 0, f_main, f_tail, f_hidden, tm, f_tail, into=u)
        (x,) = _down_residual(u, w_d, 0, 0, 2, x, None, tm, min(512, d_model))
        if l + 1 < depth:
            x, xg, ss = _down_residual(u, w_d, 0, 1, 2, x, g_mix[l + 1], tm, min(512, d_model))
        else:
            (x,) = _down_residual(u, w_d, 0, 1, 2, x, None, tm, min(512, d_model))
    return x.reshape(bsz, t, d_model).astype(out_dtype)
```

```python
import functools

import jax
import jax.numpy as jnp
from jax import lax
from jax.experimental import pallas as pl
from jax.experimental.pallas import tpu as pltpu

F32 = jnp.float32
BF16 = jnp.bfloat16

LANES = 128
V7X_VMEM_LIMIT_BYTES = 56 * 1024 * 1024

NORM_EPS = 1e-6
GLA_HEADS = 4
GLA_GATE_RANK = 16
GLA_GATE_TEMP = 16.0
GLA_CHUNK = 64
SB_HEAD_DIM = 128
SB_ZERO_WEIGHT_CARRY = 105.0


def _params(*sem):
    return pltpu.CompilerParams(dimension_semantics=sem,
                                vmem_limit_bytes=V7X_VMEM_LIMIT_BYTES)


def _dot(a, b):
    return jnp.dot(a, b, preferred_element_type=F32)


def _dot_nt(a, b):
    return lax.dot_general(a, b, (((1,), (1,)), ((), ())), preferred_element_type=F32)


def _dot_tn(a, b):
    return lax.dot_general(a, b, (((0,), (0,)), ((), ())), preferred_element_type=F32)


def _split_bf16(a):
    hi = a.astype(BF16)
    lo = (a - hi.astype(F32)).astype(BF16)
    return hi, lo


def _softplus(z):
    return jnp.maximum(z, 0.0) + jnp.log(1.0 + jnp.exp(-jnp.abs(z)))


def _rms_rows(x, g):
    return x * lax.rsqrt(jnp.mean(x * x, axis=-1, keepdims=True) + NORM_EPS) * g


def _row_scale(ss_ref, width):
    return lax.rsqrt(ss_ref[...] * (1.0 / width) + NORM_EPS)


def _emit_norm_inputs(x, g_ref, xg_ref, ss_ref, first):
    xg_ref[...] = (x * g_ref[...]).astype(xg_ref.dtype)
    part = jnp.sum(x * x, axis=-1, keepdims=True)

    @pl.when(first)
    def _():
        ss_ref[...] = part

    @pl.when(jnp.logical_not(first))
    def _():
        ss_ref[...] += part


def _norm_out(m, n, tm, tn, idx):
    specs = [pl.BlockSpec((tm, tn), idx), pl.BlockSpec((tm, 1), lambda i, *_: (i, 0))]
    shapes = [jax.ShapeDtypeStruct((m, n), BF16), jax.ShapeDtypeStruct((m, 1), F32)]
    return specs, shapes


def _prenorm_body(x_ref, g_ref, xg_ref, ss_ref):
    x = x_ref[...]
    xg_ref[...] = (x * g_ref[...]).astype(xg_ref.dtype)
    ss_ref[...] = jnp.sum(x * x, axis=-1, keepdims=True)


def _prenorm(x, g, tm):
    n, d = x.shape
    specs, shapes = _norm_out(n, d, tm, d, lambda i: (i, 0))
    return pl.pallas_call(
        _prenorm_body,
        grid=(n // tm,),
        in_specs=[pl.BlockSpec((tm, d), lambda i: (i, 0)),
                  pl.BlockSpec((1, d), lambda i: (0, 0))],
        out_specs=specs,
        out_shape=shapes,
        compiler_params=_params("parallel"),
        name="prenorm",
    )(x, g)


def _cast_specs(src, layer, rows, cols, row_blocks, col_blocks, idx):
    br, bc = rows // row_blocks, cols // col_blocks
    assert br * row_blocks == rows and bc * col_blocks == cols and br % 16 == 0 and bc % LANES == 0
    return (pl.BlockSpec((None, br, bc), lambda *g: (layer,) + idx(*g)),
            pl.BlockSpec((None, br, bc), lambda *g: (0,) + idx(*g)),
            jax.ShapeDtypeStruct((1, rows, cols), BF16))


def _mm_body(a_ref, ss_ref, w_ref, src_ref, o_ref, dst_ref, *, width):
    o_ref[...] = (_dot(a_ref[...], w_ref[...]) * _row_scale(ss_ref, width)).astype(o_ref.dtype)
    dst_ref[...] = src_ref[...].astype(dst_ref.dtype)


def _matmul_scaled(a, ss, w, layer, ncols, tm, tn, cast_src, cast_cols, name):
    m, k = a.shape
    assert ncols % tn == 0
    grid = (m // tm, ncols // tn)
    c_in, c_out, c_shape = _cast_specs(cast_src, layer, cast_src.shape[1], cast_cols, *grid,
                                       lambda i, j: (i, j))
    return pl.pallas_call(
        functools.partial(_mm_body, width=k),
        grid=grid,
        in_specs=[pl.BlockSpec((tm, k), lambda i, j: (i, 0)),
                  pl.BlockSpec((tm, 1), lambda i, j: (i, 0)),
                  pl.BlockSpec((None, k, tn), lambda i, j: (layer, 0, j)),
                  c_in],
        out_specs=[pl.BlockSpec((tm, tn), lambda i, j: (i, j)), c_out],
        out_shape=[jax.ShapeDtypeStruct((m, ncols), BF16), c_shape],
        compiler_params=_params("parallel", "arbitrary"),
        name=name,
    )(a, ss, w, cast_src)


def _mm_gate_body(a_ref, ss_ref, w_ref, wgate_ref, src_ref, o_ref, gate_ref, dst_ref, *, width):
    a = a_ref[...]
    rs = _row_scale(ss_ref, width)
    o_ref[...] = (_dot(a, w_ref[...]) * rs).astype(o_ref.dtype)
    dst_ref[...] = src_ref[...].astype(dst_ref.dtype)

    @pl.when(pl.program_id(1) == 0)
    def _():
        gate_ref[...] = _dot(a, wgate_ref[...]) * rs


def _matmul_scaled_with_gate(a, ss, w, layer, ncols, gate_col0, tm, tn, cast_src, cast_cols):
    m, k = a.shape
    assert ncols % tn == 0 and gate_col0 % LANES == 0
    gate_blk = gate_col0 // LANES
    grid = (m // tm, ncols // tn)
    c_in, c_out, c_shape = _cast_specs(cast_src, layer, cast_src.shape[1], cast_cols, *grid,
                                       lambda i, j: (i, j))
    return pl.pallas_call(
        functools.partial(_mm_gate_body, width=k),
        grid=grid,
        in_specs=[pl.BlockSpec((tm, k), lambda i, j: (i, 0)),
                  pl.BlockSpec((tm, 1), lambda i, j: (i, 0)),
                  pl.BlockSpec((None, k, tn), lambda i, j: (layer, 0, j)),
                  pl.BlockSpec((None, k, LANES), lambda i, j: (layer, 0, gate_blk)),
                  c_in],
        out_specs=[pl.BlockSpec((tm, tn), lambda i, j: (i, j)),
                   pl.BlockSpec((tm, LANES), lambda i, j: (i, 0)),
                   c_out],
        out_shape=[jax.ShapeDtypeStruct((m, ncols), BF16),
                   jax.ShapeDtypeStruct((m, LANES), F32),
                   c_shape],
        compiler_params=_params("parallel", "arbitrary"),
        name="in_proj_gla",
    )(a, ss, w, w, cast_src)


def _mm2_res_body(a1_ref, a2_ref, w1_ref, w2_ref, r_ref, g_ref, o_ref, xg_ref, ss_ref):
    x = r_ref[...] + (_dot(a1_ref[...], w1_ref[...]) + _dot(a2_ref[...], w2_ref[...]))
    o_ref[...] = x
    _emit_norm_inputs(x, g_ref, xg_ref, ss_ref, pl.program_id(1) == 0)


def _matmul2_residual(a1, a2, w, layer, res, g, tm, tn):
    m, kh = a1.shape
    assert a2.shape == a1.shape and w.shape[1] == 2 * kh
    n = w.shape[2]
    nspecs, nshapes = _norm_out(m, n, tm, tn, lambda i, j: (i, j))
    return pl.pallas_call(
        _mm2_res_body,
        grid=(m // tm, n // tn),
        in_specs=[pl.BlockSpec((tm, kh), lambda i, j: (i, 0)),
                  pl.BlockSpec((tm, kh), lambda i, j: (i, 0)),
                  pl.BlockSpec((None, kh, tn), lambda i, j: (layer, 0, j)),
                  pl.BlockSpec((None, kh, tn), lambda i, j: (layer, 1, j)),
                  pl.BlockSpec((tm, tn), lambda i, j: (i, j)),
                  pl.BlockSpec((1, tn), lambda i, j: (0, j))],
        out_specs=[pl.BlockSpec((tm, tn), lambda i, j: (i, j))] + nspecs,
        out_shape=[jax.ShapeDtypeStruct((m, n), F32)] + nshapes,
        compiler_params=_params("parallel", "arbitrary"),
        name="out_proj",
    )(a1, a2, w, w, res, g)


def _swiglu_body(a_ref, ss_ref, wg_ref, wu_ref, *rest, width):
    o_ref = rest[-1]
    a = a_ref[...]
    rs = _row_scale(ss_ref, width)
    g = _dot(a, wg_ref[...]) * rs
    u = _dot(a, wu_ref[...]) * rs
    o_ref[...] = (g / (1.0 + jnp.exp(-g)) * u).astype(o_ref.dtype)


def _swiglu(a, ss, wg, wu, layer, w_col0, out_col0, ncols, n_out, tm, tn, into=None):
    m, k = a.shape
    assert w_col0 % tn == 0 and out_col0 % tn == 0 and ncols % tn == 0
    w_blk0, o_blk0 = w_col0 // tn, out_col0 // tn
    in_specs = [pl.BlockSpec((tm, k), lambda i, j: (i, 0)),
                pl.BlockSpec((tm, 1), lambda i, j: (i, 0)),
                pl.BlockSpec((None, k, tn), lambda i, j: (layer, 0, w_blk0 + j)),
                pl.BlockSpec((None, k, tn), lambda i, j: (layer, 0, w_blk0 + j))]
    args = [a, ss, wg, wu]
    aliases = {}
    if into is not None:
        in_specs.append(pl.BlockSpec(memory_space=pl.ANY))
        aliases = {len(args): 0}
        args.append(into)
    return pl.pallas_call(
        functools.partial(_swiglu_body, width=k),
        grid=(m // tm, ncols // tn),
        in_specs=in_specs,
        out_specs=pl.BlockSpec((tm, tn), lambda i, j: (i, o_blk0 + j)),
        out_shape=jax.ShapeDtypeStruct((m, n_out), BF16),
        input_output_aliases=aliases,
        compiler_params=_params("parallel", "arbitrary"),
        name="ffn_gate_up",
    )(*args)


def _down_body(a_ref, w_ref, r_ref, o_ref):
    o_ref[...] = r_ref[...] + _dot(a_ref[...], w_ref[...])


def _down_norm_body(a_ref, w_ref, r_ref, g_ref, o_ref, xg_ref, ss_ref):
    x = r_ref[...] + _dot(a_ref[...], w_ref[...])
    o_ref[...] = x
    _emit_norm_inputs(x, g_ref, xg_ref, ss_ref, pl.program_id(1) == 0)


def _down_residual(a, w, layer, part, parts, res, g, tm, tn):
    m = a.shape[0]
    n = w.shape[2]
    k = w.shape[1] // parts
    assert k * parts == w.shape[1] and k % LANES == 0
    in_specs = [pl.BlockSpec((tm, k), lambda i, j: (i, part)),
                pl.BlockSpec((None, k, tn), lambda i, j: (layer, part, j)),
                pl.BlockSpec((tm, tn), lambda i, j: (i, j))]
    out_specs = [pl.BlockSpec((tm, tn), lambda i, j: (i, j))]
    out_shape = [jax.ShapeDtypeStruct((m, n), F32)]
    args = [a, w, res]
    body = _down_body
    if g is not None:
        nspecs, nshapes = _norm_out(m, n, tm, tn, lambda i, j: (i, j))
        in_specs.append(pl.BlockSpec((1, tn), lambda i, j: (0, j)))
        out_specs += nspecs
        out_shape += nshapes
        args.append(g)
        body = _down_norm_body
    return pl.pallas_call(
        body,
        grid=(m // tm, n // tn),
        in_specs=in_specs,
        out_specs=out_specs,
        out_shape=out_shape,
        compiler_params=_params("parallel", "arbitrary"),
        name="ffn_down",
    )(*args)


def _gla_body(q_ref, k_ref, v_ref, r_ref, lr_ref, wup_ref, bg_ref, gn_ref, src_ref, o_ref, dst_ref,
              st_ref, *, n_chunks, heads, dk, dv):
    dst_ref[...] = src_ref[...].astype(dst_ref.dtype)

    @pl.when(pl.program_id(1) == 0)
    def _():
        st_ref[...] = jnp.zeros_like(st_ref)

    c = GLA_CHUNK
    row = lax.broadcasted_iota(jnp.int32, (c, c), 0)
    col = lax.broadcasted_iota(jnp.int32, (c, c), 1)
    causal = col <= row
    tri = causal.astype(BF16)
    tri2 = jnp.concatenate([tri, tri], axis=1)
    scale = dk ** -0.5

    for ci in range(n_chunks):
        rows = pl.ds(ci * c, c)
        logits = _dot(lr_ref[rows, :].astype(BF16), wup_ref[...]) + bg_ref[...]
        log_alpha = -_softplus(-logits) / GLA_GATE_TEMP
        b_all = _dot(tri2, jnp.concatenate(_split_bf16(log_alpha), axis=0))
        for h in range(heads):
            ks = pl.ds(h * dk, dk)
            vs = pl.ds(h * dv, dv)
            b = b_all[:, h * dk:(h + 1) * dk]
            b_last = b[c - 1:c, :]
            q = q_ref[rows, ks].astype(F32)
            k = k_ref[rows, ks].astype(F32)
            v = v_ref[rows, vs]
            q_dec = (q * scale * jnp.exp(b)).astype(BF16)
            k_intra = (k * jnp.exp(-b)).astype(BF16)
            k_state = (k * jnp.exp(b_last - b)).astype(BF16)

            scores = jnp.where(causal, _dot_nt(q_dec, k_intra), 0.0)
            st = st_ref[h]
            o = _dot(scores.astype(BF16), v) + _dot_nt(q_dec, st.astype(BF16))
            st_ref[h] = st * jnp.exp(b_last) + _dot_tn(v, k_state)

            r = r_ref[rows, vs].astype(F32)
            o = _rms_rows(o, gn_ref[...]) * (r / (1.0 + jnp.exp(-r)))
            o_ref[rows, vs] = o.astype(o_ref.dtype)


def _gla(proj, lr, wup, bg, gn, cast_src, layer, *, heads, dk, dv, rows):
    bsz, t, _ = proj.shape
    kw = heads * dk
    vw = heads * dv
    assert (2 * kw) % vw == 0
    v_blk = (2 * kw) // vw
    body = functools.partial(_gla_body, n_chunks=rows // GLA_CHUNK, heads=heads, dk=dk, dv=dv)
    c_in, c_out, c_shape = _cast_specs(cast_src, layer, cast_src.shape[1], cast_src.shape[2],
                                       bsz, t // rows, lambda b, i: (b, i))
    return pl.pallas_call(
        body,
        grid=(bsz, t // rows),
        in_specs=[
            pl.BlockSpec((None, rows, kw), lambda b, i: (b, i, 0)),
            pl.BlockSpec((None, rows, kw), lambda b, i: (b, i, 1)),
            pl.BlockSpec((None, rows, vw), lambda b, i: (b, i, v_blk)),
            pl.BlockSpec((None, rows, vw), lambda b, i: (b, i, v_blk + 1)),
            pl.BlockSpec((None, rows, LANES), lambda b, i: (b, i, 0)),
            pl.BlockSpec((LANES, kw), lambda b, i: (0, 0)),
            pl.BlockSpec((1, kw), lambda b, i: (0, 0)),
            pl.BlockSpec((1, dv), lambda b, i: (0, 0)),
            c_in,
        ],
        out_specs=[pl.BlockSpec((None, rows, vw), lambda b, i: (b, i, 0)), c_out],
        out_shape=[jax.ShapeDtypeStruct((bsz, t, vw), BF16), c_shape],
        scratch_shapes=[pltpu.VMEM((heads, dv, dk), F32)],
        compiler_params=_params("parallel", "arbitrary"),
        name="gla",
    )(proj, proj, proj, proj, lr, wup, bg, gn, cast_src)


def _sb_body(q_ref, k_ref, v_ref, gq_ref, gk_ref, src_ref, o_ref, dst_ref, kn_ref, qn_ref, acc_ref,
             carry_ref, lt_ref, *, tq, d, group):
    qi = pl.program_id(2)
    heads = [pl.ds(g * d, d) for g in range(group)]
    sub = tq // d
    dst_ref[...] = src_ref[...].astype(dst_ref.dtype)

    @pl.when(qi == 0)
    def _():
        for hs in heads:
            kn_ref[:, hs] = _rms_rows(k_ref[:, hs].astype(F32), gk_ref[...]).astype(kn_ref.dtype)
        j = lax.broadcasted_iota(jnp.int32, (2 * d, 2 * d), 0) % d
        s = lax.broadcasted_iota(jnp.int32, (2 * d, 2 * d), 1)
        lt_ref[...] = ((j > s) | (s >= d)).astype(lt_ref.dtype)

    scale = d ** -0.5
    for hs in heads:
        qn_ref[:, hs] = (_rms_rows(q_ref[:, hs].astype(F32), gq_ref[...]) * scale).astype(qn_ref.dtype)
    acc_ref[...] = jnp.zeros_like(acc_ref)
    carry_ref[...] = jnp.zeros_like(carry_ref)

    def key_block(base, masks):
        tiles = [(hs, s) for hs in heads for s in reversed(range(sub))]
        blk = {s: pl.ds(pl.multiple_of(base + s * d, d), d) for s in range(sub)}
        z = [_dot_nt(qn_ref[:, hs], kn_ref[blk[s], hs]) for hs, s in tiles]
        sp = [_softplus(zt) for zt in z]
        if masks is None:
            keep = sp
        else:
            keep = [jnp.where(masks[s], spt, 0.0) for spt, (_, s) in zip(sp, tiles)]
        sums = [_dot(jnp.concatenate(_split_bf16(kt), axis=1), lt_ref[...]) for kt in keep]
        low = None
        it = iter(range(len(tiles)))
        for hs in heads:
            carry = carry_ref[:, hs]
            out = None
            for s in reversed(range(sub)):
                t = next(it)
                w = jnp.exp(z[t] - sp[t] - sums[t][:, :d] - carry)
                if masks is not None:
                    w = jnp.where(masks[s], w, 0.0)
                part = _dot(w.astype(BF16), v_ref[blk[s], hs])
                out = part if out is None else out + part
                carry = carry + sums[t][:, d:]
            carry_ref[:, hs] = carry
            acc_ref[:, hs] += out
            low = jnp.min(carry) if low is None else jnp.minimum(low, jnp.min(carry))
        return low

    row = lax.broadcasted_iota(jnp.int32, (tq, d), 0)
    col = lax.broadcasted_iota(jnp.int32, (tq, d), 1)
    low = key_block(qi * tq, [col + s * d < row for s in range(sub)])

    def more(state):
        n, low = state
        return jnp.logical_and(n < qi, low < SB_ZERO_WEIGHT_CARRY)

    def step(state):
        n, _ = state
        return n + 1, key_block((qi - 1 - n) * tq, None)

    lax.while_loop(more, step, (jnp.int32(0), low))
    o_ref[...] = acc_ref[...].astype(o_ref.dtype)


def _stick_breaking(proj, gq, gk, cast_src, layer, *, heads, d, col0, tq, group):
    bsz, t, _ = proj.shape
    assert d == LANES and tq % d == 0
    gw = group * d
    q0 = col0 // gw
    k0 = q0 + heads // group
    v0 = k0 + heads // group
    nq = t // tq
    body = functools.partial(_sb_body, tq=tq, d=d, group=group)
    c_in, c_out, c_shape = _cast_specs(cast_src, layer, cast_src.shape[1], cast_src.shape[2],
                                       bsz, heads // group * nq, lambda b, h, i: (b, h * nq + i))
    return pl.pallas_call(
        body,
        grid=(bsz, heads // group, nq),
        in_specs=[
            pl.BlockSpec((None, tq, gw), lambda b, h, i: (b, i, q0 + h)),
            pl.BlockSpec((None, t, gw), lambda b, h, i: (b, 0, k0 + h)),
            pl.BlockSpec((None, t, gw), lambda b, h, i: (b, 0, v0 + h)),
            pl.BlockSpec((1, d), lambda b, h, i: (0, 0)),
            pl.BlockSpec((1, d), lambda b, h, i: (0, 0)),
            c_in,
        ],
        out_specs=[pl.BlockSpec((None, tq, gw), lambda b, h, i: (b, i, h)), c_out],
        out_shape=[jax.ShapeDtypeStruct((bsz, t, heads * d), BF16), c_shape],
        scratch_shapes=[pltpu.VMEM((t, gw), BF16),
                        pltpu.VMEM((tq, gw), BF16),
                        pltpu.VMEM((tq, gw), F32),
                        pltpu.VMEM((tq, gw), F32),
                        pltpu.VMEM((2 * d, 2 * d), BF16)],
        compiler_params=_params("parallel", "parallel", "arbitrary"),
        name="stick_breaking",
    )(proj, proj, proj, gq, gk, cast_src)


def _pad_to(a, axis, size):
    pad = [(0, 0)] * a.ndim
    pad[axis] = (0, size - a.shape[axis])
    return jnp.pad(a, pad)


def kernel(x, norm_mix, w_in, w_gate_up, b_gate, gla_o_norm, sb_q_norm, sb_k_norm, w_out,
           norm_ffn, w_ffn_gate, w_ffn_up, w_ffn_down):
    bsz, t, d_model = x.shape
    depth = w_in.shape[0]
    n = bsz * t
    gla_kw = w_gate_up.shape[2]
    gla_dv = gla_o_norm.shape[1]
    gla_w = GLA_HEADS * gla_dv
    gla_dk = gla_kw // GLA_HEADS
    lr0 = 2 * gla_kw + 2 * gla_w
    sb_w = (w_in.shape[2] - lr0 - GLA_GATE_RANK) // 3
    sb_heads = sb_w // SB_HEAD_DIM
    f_hidden = w_ffn_gate.shape[2]
    tn_ffn = min(512, f_hidden)
    f_main = f_hidden // tn_ffn * tn_ffn

    out_dtype = x.dtype
    x = x.astype(F32).reshape(n, d_model)

    w_in_bf = w_in.astype(BF16)
    w_sb = w_in_bf[:, :, lr0 + GLA_GATE_RANK:]
    w_up = _pad_to(w_gate_up, 1, LANES).astype(BF16)
    w_g_tail = w_ffn_gate[:, :, f_main:].astype(BF16)
    w_u_tail = w_ffn_up[:, :, f_main:].astype(BF16)
    g_mix = norm_mix.reshape(depth, 1, d_model)
    g_ffn = norm_ffn.reshape(depth, 1, d_model)

    tm = min(1024, n)
    xg, ss = _prenorm(x, g_mix[0], min(512, n))
    for l in range(depth):
        proj_gla, lr, w_g = _matmul_scaled_with_gate(xg, ss, w_in_bf, l, lr0, lr0, tm, min(1024, lr0),
                                                     w_ffn_gate, f_main)
        proj_sb, w_u = _matmul_scaled(xg, ss, w_sb, l, 3 * sb_w, tm, min(1024, 3 * sb_w),
                                      w_ffn_up, f_main, "in_proj_sb")
        o_a, w_o = _gla(proj_gla.reshape(bsz, t, lr0), lr.reshape(bsz, t, LANES), w_up[l],
                        b_gate[l].reshape(1, -1), gla_o_norm[l].reshape(1, -1), w_out, l,
                        heads=GLA_HEADS, dk=gla_dk, dv=gla_dv, rows=min(256, t))
        o_b, w_d = _stick_breaking(proj_sb.reshape(bsz, t, 3 * sb_w), sb_q_norm[l].reshape(1, -1),
                                   sb_k_norm[l].reshape(1, -1), w_ffn_down, l, heads=sb_heads,
                                   d=SB_HEAD_DIM, col0=0, tq=min(256, t), group=8)
        x, xg, ss = _matmul2_residual(o_a.reshape(n, gla_w), o_b.reshape(n, sb_w), w_o, 0, x,
                                      g_ffn[l], tm, min(512, d_model))

        u = _swiglu(xg, ss, w_g, w_u, 0, 0, 0, f_main, f_hidden, tm, tn_ffn)
        if f_main < f_hidden:
            f_tail = f_hidden - f_main
            u = _swiglu(xg, ss, w_g_tail, w_u_tail, l, 0, f_main, f_tail, f_hidden, tm, f_tail, into=u)
        (x,) = _down_residual(u, w_d, 0, 0, 2, x, None, tm, min(512, d_model))
        if l + 1 < depth:
            x, xg, ss = _down_residual(u, w_d, 0, 1, 2, x, g_mix[l + 1], tm, min(512, d_model))
        else:
            (x,) = _down_residual(u, w_d, 0, 1, 2, x, None, tm, min(512, d_model))
    return x.reshape(bsz, t, d_model).astype(out_dtype)
```

```python
import functools

import jax
import jax.numpy as jnp
from jax import lax
from jax.experimental import pallas as pl
from jax.experimental.pallas import tpu as pltpu

F32 = jnp.float32
BF16 = jnp.bfloat16

LANES = 128
V7X_VMEM_LIMIT_BYTES = 56 * 1024 * 1024

NORM_EPS = 1e-6
GLA_HEADS = 4
GLA_GATE_RANK = 16
GLA_GATE_TEMP = 16.0
GLA_CHUNK = 64
SB_HEAD_DIM = 128
SB_ZERO_WEIGHT_CARRY = 105.0

TOKEN_TILE = 1024
PRENORM_ROWS = 512
IN_PROJ_COLS = 1024
OUT_PROJ_COLS = 512
FFN_COLS = 512
FFN_DOWN_PARTS = 2
MIXER_ROWS = 256
SB_HEAD_GROUP = 8


def _params(*sem):
    return pltpu.CompilerParams(dimension_semantics=sem,
                                vmem_limit_bytes=V7X_VMEM_LIMIT_BYTES)


def _dot(a, b):
    return jnp.dot(a, b, preferred_element_type=F32)


def _dot_nt(a, b):
    return lax.dot_general(a, b, (((1,), (1,)), ((), ())), preferred_element_type=F32)


def _dot_tn(a, b):
    return lax.dot_general(a, b, (((0,), (0,)), ((), ())), preferred_element_type=F32)


def _split_bf16(a):
    hi = a.astype(BF16)
    lo = (a - hi.astype(F32)).astype(BF16)
    return hi, lo


def _softplus(z):
    return jnp.maximum(z, 0.0) + jnp.log(1.0 + jnp.exp(-jnp.abs(z)))


def _rms_rows(x, g):
    return x * lax.rsqrt(jnp.mean(x * x, axis=-1, keepdims=True) + NORM_EPS) * g


def _row_scale(ss_ref, width):
    return lax.rsqrt(ss_ref[...] * (1.0 / width) + NORM_EPS)


def _emit_norm_inputs(x, g_ref, xg_ref, ss_ref, first):
    xg_ref[...] = (x * g_ref[...]).astype(xg_ref.dtype)
    part = jnp.sum(x * x, axis=-1, keepdims=True)

    @pl.when(first)
    def _():
        ss_ref[...] = part

    @pl.when(jnp.logical_not(first))
    def _():
        ss_ref[...] += part


def _norm_out(m, n, tm, tn, idx):
    specs = [pl.BlockSpec((tm, tn), idx), pl.BlockSpec((tm, 1), lambda i, *_: (i, 0))]
    shapes = [jax.ShapeDtypeStruct((m, n), BF16), jax.ShapeDtypeStruct((m, 1), F32)]
    return specs, shapes


def _prenorm_body(x_ref, g_ref, xg_ref, ss_ref):
    x = x_ref[...]
    xg_ref[...] = (x * g_ref[...]).astype(xg_ref.dtype)
    ss_ref[...] = jnp.sum(x * x, axis=-1, keepdims=True)


def _prenorm(x, g, tm):
    n, d = x.shape
    specs, shapes = _norm_out(n, d, tm, d, lambda i: (i, 0))
    return pl.pallas_call(
        _prenorm_body,
        grid=(n // tm,),
        in_specs=[pl.BlockSpec((tm, d), lambda i: (i, 0)),
                  pl.BlockSpec((1, d), lambda i: (0, 0))],
        out_specs=specs,
        out_shape=shapes,
        compiler_params=_params("parallel"),
        name="prenorm",
    )(x, g)


def _cast_specs(src, layer, rows, cols, row_blocks, col_blocks, idx):
    br, bc = rows // row_blocks, cols // col_blocks
    assert br * row_blocks == rows and bc * col_blocks == cols and br % 16 == 0 and bc % LANES == 0
    return (pl.BlockSpec((None, br, bc), lambda *g: (layer,) + idx(*g)),
            pl.BlockSpec((None, br, bc), lambda *g: (0,) + idx(*g)),
            jax.ShapeDtypeStruct((1, rows, cols), BF16))


def _mm_body(a_ref, ss_ref, w_ref, src_ref, o_ref, dst_ref, *, width):
    o_ref[...] = (_dot(a_ref[...], w_ref[...]) * _row_scale(ss_ref, width)).astype(o_ref.dtype)
    dst_ref[...] = src_ref[...].astype(dst_ref.dtype)


def _matmul_scaled(a, ss, w, layer, ncols, tm, tn, cast_src, cast_cols, name):
    m, k = a.shape
    assert ncols % tn == 0
    grid = (m // tm, ncols // tn)
    c_in, c_out, c_shape = _cast_specs(cast_src, layer, cast_src.shape[1], cast_cols, *grid,
                                       lambda i, j: (i, j))
    return pl.pallas_call(
        functools.partial(_mm_body, width=k),
        grid=grid,
        in_specs=[pl.BlockSpec((tm, k), lambda i, j: (i, 0)),
                  pl.BlockSpec((tm, 1), lambda i, j: (i, 0)),
                  pl.BlockSpec((None, k, tn), lambda i, j: (layer, 0, j)),
                  c_in],
        out_specs=[pl.BlockSpec((tm, tn), lambda i, j: (i, j)), c_out],
        out_shape=[jax.ShapeDtypeStruct((m, ncols), BF16), c_shape],
        compiler_params=_params("parallel", "arbitrary"),
        name=name,
    )(a, ss, w, cast_src)


def _mm_gate_body(a_ref, ss_ref, w_ref, wgate_ref, src_ref, o_ref, gate_ref, dst_ref, *, width):
    a = a_ref[...]
    rs = _row_scale(ss_ref, width)
    o_ref[...] = (_dot(a, w_ref[...]) * rs).astype(o_ref.dtype)
    dst_ref[...] = src_ref[...].astype(dst_ref.dtype)

    @pl.when(pl.program_id(1) == 0)
    def _():
        gate_ref[...] = _dot(a, wgate_ref[...]) * rs


def _matmul_scaled_with_gate(a, ss, w, layer, ncols, gate_col0, tm, tn, cast_src, cast_cols):
    m, k = a.shape
    assert ncols % tn == 0 and gate_col0 % LANES == 0
    gate_blk = gate_col0 // LANES
    grid = (m // tm, ncols // tn)
    c_in, c_out, c_shape = _cast_specs(cast_src, layer, cast_src.shape[1], cast_cols, *grid,
                                       lambda i, j: (i, j))
    return pl.pallas_call(
        functools.partial(_mm_gate_body, width=k),
        grid=grid,
        in_specs=[pl.BlockSpec((tm, k), lambda i, j: (i, 0)),
                  pl.BlockSpec((tm, 1), lambda i, j: (i, 0)),
                  pl.BlockSpec((None, k, tn), lambda i, j: (layer, 0, j)),
                  pl.BlockSpec((None, k, LANES), lambda i, j: (layer, 0, gate_blk)),
                  c_in],
        out_specs=[pl.BlockSpec((tm, tn), lambda i, j: (i, j)),
                   pl.BlockSpec((tm, LANES), lambda i, j: (i, 0)),
                   c_out],
        out_shape=[jax.ShapeDtypeStruct((m, ncols), BF16),
                   jax.ShapeDtypeStruct((m, LANES), F32),
                   c_shape],
        compiler_params=_params("parallel", "arbitrary"),
        name="in_proj_gla",
    )(a, ss, w, w, cast_src)


def _mm2_res_body(a1_ref, a2_ref, w1_ref, w2_ref, r_ref, g_ref, o_ref, xg_ref, ss_ref):
    x = r_ref[...] + (_dot(a1_ref[...], w1_ref[...]) + _dot(a2_ref[...], w2_ref[...]))
    o_ref[...] = x
    _emit_norm_inputs(x, g_ref, xg_ref, ss_ref, pl.program_id(1) == 0)


def _matmul2_residual(a1, a2, w, layer, res, g, tm, tn):
    m, kh = a1.shape
    assert a2.shape == a1.shape and w.shape[1] == 2 * kh
    n = w.shape[2]
    nspecs, nshapes = _norm_out(m, n, tm, tn, lambda i, j: (i, j))
    return pl.pallas_call(
        _mm2_res_body,
        grid=(m // tm, n // tn),
        in_specs=[pl.BlockSpec((tm, kh), lambda i, j: (i, 0)),
                  pl.BlockSpec((tm, kh), lambda i, j: (i, 0)),
                  pl.BlockSpec((None, kh, tn), lambda i, j: (layer, 0, j)),
                  pl.BlockSpec((None, kh, tn), lambda i, j: (layer, 1, j)),
                  pl.BlockSpec((tm, tn), lambda i, j: (i, j)),
                  pl.BlockSpec((1, tn), lambda i, j: (0, j))],
        out_specs=[pl.BlockSpec((tm, tn), lambda i, j: (i, j))] + nspecs,
        out_shape=[jax.ShapeDtypeStruct((m, n), F32)] + nshapes,
        compiler_params=_params("parallel", "arbitrary"),
        name="out_proj",
    )(a1, a2, w, w, res, g)


def _swiglu_body(a_ref, ss_ref, wg_ref, wu_ref, *rest, width):
    o_ref = rest[-1]
    a = a_ref[...]
    rs = _row_scale(ss_ref, width)
    g = _dot(a, wg_ref[...]) * rs
    u = _dot(a, wu_ref[...]) * rs
    o_ref[...] = (g / (1.0 + jnp.exp(-g)) * u).astype(o_ref.dtype)


def _swiglu(a, ss, wg, wu, layer, w_col0, out_col0, ncols, n_out, tm, tn, into=None):
    m, k = a.shape
    assert w_col0 % tn == 0 and out_col0 % tn == 0 and ncols % tn == 0
    w_blk0, o_blk0 = w_col0 // tn, out_col0 // tn
    in_specs = [pl.BlockSpec((tm, k), lambda i, j: (i, 0)),
                pl.BlockSpec((tm, 1), lambda i, j: (i, 0)),
                pl.BlockSpec((None, k, tn), lambda i, j: (layer, 0, w_blk0 + j)),
                pl.BlockSpec((None, k, tn), lambda i, j: (layer, 0, w_blk0 + j))]
    args = [a, ss, wg, wu]
    aliases = {}
    if into is not None:
        in_specs.append(pl.BlockSpec(memory_space=pl.ANY))
        aliases = {len(args): 0}
        args.append(into)
    return pl.pallas_call(
        functools.partial(_swiglu_body, width=k),
        grid=(m // tm, ncols // tn),
        in_specs=in_specs,
        out_specs=pl.BlockSpec((tm, tn), lambda i, j: (i, o_blk0 + j)),
        out_shape=jax.ShapeDtypeStruct((m, n_out), BF16),
        input_output_aliases=aliases,
        compiler_params=_params("parallel", "arbitrary"),
        name="ffn_gate_up",
    )(*args)


def _down_body(a_ref, w_ref, r_ref, o_ref):
    o_ref[...] = r_ref[...] + _dot(a_ref[...], w_ref[...])


def _down_norm_body(a_ref, w_ref, r_ref, g_ref, o_ref, xg_ref, ss_ref):
    x = r_ref[...] + _dot(a_ref[...], w_ref[...])
    o_ref[...] = x
    _emit_norm_inputs(x, g_ref, xg_ref, ss_ref, pl.program_id(1) == 0)


def _down_residual(a, w, layer, part, parts, res, g, tm, tn):
    m = a.shape[0]
    n = w.shape[2]
    k = w.shape[1] // parts
    assert k * parts == w.shape[1] and k % LANES == 0
    in_specs = [pl.BlockSpec((tm, k), lambda i, j: (i, part)),
                pl.BlockSpec((None, k, tn), lambda i, j: (layer, part, j)),
                pl.BlockSpec((tm, tn), lambda i, j: (i, j))]
    out_specs = [pl.BlockSpec((tm, tn), lambda i, j: (i, j))]
    out_shape = [jax.ShapeDtypeStruct((m, n), F32)]
    args = [a, w, res]
    body = _down_body
    if g is not None:
        nspecs, nshapes = _norm_out(m, n, tm, tn, lambda i, j: (i, j))
        in_specs.append(pl.BlockSpec((1, tn), lambda i, j: (0, j)))
        out_specs += nspecs
        out_shape += nshapes
        args.append(g)
        body = _down_norm_body
    return pl.pallas_call(
        body,
        grid=(m // tm, n // tn),
        in_specs=in_specs,
        out_specs=out_specs,
        out_shape=out_shape,
        compiler_params=_params("parallel", "arbitrary"),
        name="ffn_down",
    )(*args)


def _gla_body(q_ref, k_ref, v_ref, r_ref, lr_ref, wup_ref, bg_ref, gn_ref, src_ref, o_ref, dst_ref,
              st_ref, *, n_chunks, heads, dk, dv):
    dst_ref[...] = src_ref[...].astype(dst_ref.dtype)

    @pl.when(pl.program_id(1) == 0)
    def _():
        st_ref[...] = jnp.zeros_like(st_ref)

    c = GLA_CHUNK
    row = lax.broadcasted_iota(jnp.int32, (c, c), 0)
    col = lax.broadcasted_iota(jnp.int32, (c, c), 1)
    causal = col <= row
    tri = causal.astype(BF16)
    tri2 = jnp.concatenate([tri, tri], axis=1)
    scale = dk ** -0.5

    for ci in range(n_chunks):
        rows = pl.ds(ci * c, c)
        logits = _dot(lr_ref[rows, :].astype(BF16), wup_ref[...]) + bg_ref[...]
        log_alpha = -_softplus(-logits) / GLA_GATE_TEMP
        b_all = _dot(tri2, jnp.concatenate(_split_bf16(log_alpha), axis=0))
        for h in range(heads):
            ks = pl.ds(h * dk, dk)
            vs = pl.ds(h * dv, dv)
            b = b_all[:, h * dk:(h + 1) * dk]
            b_last = b[c - 1:c, :]
            q = q_ref[rows, ks].astype(F32)
            k = k_ref[rows, ks].astype(F32)
            v = v_ref[rows, vs]
            q_dec = (q * scale * jnp.exp(b)).astype(BF16)
            k_intra = (k * jnp.exp(-b)).astype(BF16)
            k_state = (k * jnp.exp(b_last - b)).astype(BF16)

            scores = jnp.where(causal, _dot_nt(q_dec, k_intra), 0.0)
            st = st_ref[h]
            o = _dot(scores.astype(BF16), v) + _dot_nt(q_dec, st.astype(BF16))
            st_ref[h] = st * jnp.exp(b_last) + _dot_tn(v, k_state)

            r = r_ref[rows, vs].astype(F32)
            o = _rms_rows(o, gn_ref[...]) * (r / (1.0 + jnp.exp(-r)))
            o_ref[rows, vs] = o.astype(o_ref.dtype)


def _gla(proj, lr, wup, bg, gn, cast_src, layer, *, heads, dk, dv, rows):
    bsz, t, _ = proj.shape
    kw = heads * dk
    vw = heads * dv
    assert (2 * kw) % vw == 0
    v_blk = (2 * kw) // vw
    body = functools.partial(_gla_body, n_chunks=rows // GLA_CHUNK, heads=heads, dk=dk, dv=dv)
    c_in, c_out, c_shape = _cast_specs(cast_src, layer, cast_src.shape[1], cast_src.shape[2],
                                       bsz, t // rows, lambda b, i: (b, i))
    return pl.pallas_call(
        body,
        grid=(bsz, t // rows),
        in_specs=[
            pl.BlockSpec((None, rows, kw), lambda b, i: (b, i, 0)),
            pl.BlockSpec((None, rows, kw), lambda b, i: (b, i, 1)),
            pl.BlockSpec((None, rows, vw), lambda b, i: (b, i, v_blk)),
            pl.BlockSpec((None, rows, vw), lambda b, i: (b, i, v_blk + 1)),
            pl.BlockSpec((None, rows, LANES), lambda b, i: (b, i, 0)),
            pl.BlockSpec((LANES, kw), lambda b, i: (0, 0)),
            pl.BlockSpec((1, kw), lambda b, i: (0, 0)),
            pl.BlockSpec((1, dv), lambda b, i: (0, 0)),
            c_in,
        ],
        out_specs=[pl.BlockSpec((None, rows, vw), lambda b, i: (b, i, 0)), c_out],
        out_shape=[jax.ShapeDtypeStruct((bsz, t, vw), BF16), c_shape],
        scratch_shapes=[pltpu.VMEM((heads, dv, dk), F32)],
        compiler_params=_params("parallel", "arbitrary"),
        name="gla",
    )(proj, proj, proj, proj, lr, wup, bg, gn, cast_src)


def _sb_body(q_ref, k_ref, v_ref, gq_ref, gk_ref, src_ref, o_ref, dst_ref, kn_ref, qn_ref, acc_ref,
             carry_ref, lt_ref, *, tq, d, group):
    qi = pl.program_id(2)
    heads = [pl.ds(g * d, d) for g in range(group)]
    sub = tq // d
    dst_ref[...] = src_ref[...].astype(dst_ref.dtype)

    @pl.when(qi == 0)
    def _():
        for hs in heads:
            kn_ref[:, hs] = _rms_rows(k_ref[:, hs].astype(F32), gk_ref[...]).astype(kn_ref.dtype)
        j = lax.broadcasted_iota(jnp.int32, (2 * d, 2 * d), 0) % d
        s = lax.broadcasted_iota(jnp.int32, (2 * d, 2 * d), 1)
        lt_ref[...] = ((j > s) | (s >= d)).astype(lt_ref.dtype)

    scale = d ** -0.5
    for hs in heads:
        qn_ref[:, hs] = (_rms_rows(q_ref[:, hs].astype(F32), gq_ref[...]) * scale).astype(qn_ref.dtype)
    acc_ref[...] = jnp.zeros_like(acc_ref)
    carry_ref[...] = jnp.zeros_like(carry_ref)

    def key_block(base, masks):
        tiles = [(hs, s) for hs in heads for s in reversed(range(sub))]
        blk = {s: pl.ds(pl.multiple_of(base + s * d, d), d) for s in range(sub)}
        z = [_dot_nt(qn_ref[:, hs], kn_ref[blk[s], hs]) for hs, s in tiles]
        sp = [_softplus(zt) for zt in z]
        if masks is None:
            keep = sp
        else:
            keep = [jnp.where(masks[s], spt, 0.0) for spt, (_, s) in zip(sp, tiles)]
        sums = [_dot(jnp.concatenate(_split_bf16(kt), axis=1), lt_ref[...]) for kt in keep]
        low = None
        it = iter(range(len(tiles)))
        for hs in heads:
            carry = carry_ref[:, hs]
            out = None
            for s in reversed(range(sub)):
                t = next(it)
                w = jnp.exp(z[t] - sp[t] - sums[t][:, :d] - carry)
                if masks is not None:
                    w = jnp.where(masks[s], w, 0.0)
                part = _dot(w.astype(BF16), v_ref[blk[s], hs])
                out = part if out is None else out + part
                carry = carry + sums[t][:, d:]
            carry_ref[:, hs] = carry
            acc_ref[:, hs] += out
            low = jnp.min(carry) if low is None else jnp.minimum(low, jnp.min(carry))
        return low

    row = lax.broadcasted_iota(jnp.int32, (tq, d), 0)
    col = lax.broadcasted_iota(jnp.int32, (tq, d), 1)
    low = key_block(qi * tq, [col + s * d < row for s in range(sub)])

    def more(state):
        n, low = state
        return jnp.logical_and(n < qi, low < SB_ZERO_WEIGHT_CARRY)

    def step(state):
        n, _ = state
        return n + 1, key_block((qi - 1 - n) * tq, None)

    lax.while_loop(more, step, (jnp.int32(0), low))
    o_ref[...] = acc_ref[...].astype(o_ref.dtype)


def _stick_breaking(proj, gq, gk, cast_src, layer, *, heads, d, col0, tq, group):
    bsz, t, _ = proj.shape
    assert d == LANES and tq % d == 0
    gw = group * d
    q0 = col0 // gw
    k0 = q0 + heads // group
    v0 = k0 + heads // group
    nq = t // tq
    body = functools.partial(_sb_body, tq=tq, d=d, group=group)
    c_in, c_out, c_shape = _cast_specs(cast_src, layer, cast_src.shape[1], cast_src.shape[2],
                                       bsz, heads // group * nq, lambda b, h, i: (b, h * nq + i))
    return pl.pallas_call(
        body,
        grid=(bsz, heads // group, nq),
        in_specs=[
            pl.BlockSpec((None, tq, gw), lambda b, h, i: (b, i, q0 + h)),
            pl.BlockSpec((None, t, gw), lambda b, h, i: (b, 0, k0 + h)),
            pl.BlockSpec((None, t, gw), lambda b, h, i: (b, 0, v0 + h)),
            pl.BlockSpec((1, d), lambda b, h, i: (0, 0)),
            pl.BlockSpec((1, d), lambda b, h, i: (0, 0)),
            c_in,
        ],
        out_specs=[pl.BlockSpec((None, tq, gw), lambda b, h, i: (b, i, h)), c_out],
        out_shape=[jax.ShapeDtypeStruct((bsz, t, heads * d), BF16), c_shape],
        scratch_shapes=[pltpu.VMEM((t, gw), BF16),
                        pltpu.VMEM((tq, gw), BF16),
                        pltpu.VMEM((tq, gw), F32),
                        pltpu.VMEM((tq, gw), F32),
                        pltpu.VMEM((2 * d, 2 * d), BF16)],
        compiler_params=_params("parallel", "parallel", "arbitrary"),
        name="stick_breaking",
    )(proj, proj, proj, gq, gk, cast_src)


def _pad_to(a, axis, size):
    pad = [(0, 0)] * a.ndim
    pad[axis] = (0, size - a.shape[axis])
    return jnp.pad(a, pad)


def kernel(x, norm_mix, w_in, w_gate_up, b_gate, gla_o_norm, sb_q_norm, sb_k_norm, w_out,
           norm_ffn, w_ffn_gate, w_ffn_up, w_ffn_down):
    bsz, t, d_model = x.shape
    depth = w_in.shape[0]
    n = bsz * t
    gla_kw = w_gate_up.shape[2]
    gla_dv = gla_o_norm.shape[1]
    gla_w = GLA_HEADS * gla_dv
    gla_dk = gla_kw // GLA_HEADS
    lr0 = 2 * gla_kw + 2 * gla_w
    sb_w = (w_in.shape[2] - lr0 - GLA_GATE_RANK) // 3
    sb_heads = sb_w // SB_HEAD_DIM
    f_hidden = w_ffn_gate.shape[2]
    tn_ffn = min(FFN_COLS, f_hidden)
    f_main = f_hidden // tn_ffn * tn_ffn

    out_dtype = x.dtype
    x = x.astype(F32).reshape(n, d_model)

    w_in_bf = w_in.astype(BF16)
    w_sb = w_in_bf[:, :, lr0 + GLA_GATE_RANK:]
    w_up = _pad_to(w_gate_up, 1, LANES).astype(BF16)
    w_g_tail = w_ffn_gate[:, :, f_main:].astype(BF16)
    w_u_tail = w_ffn_up[:, :, f_main:].astype(BF16)
    g_mix = norm_mix.reshape(depth, 1, d_model)
    g_ffn = norm_ffn.reshape(depth, 1, d_model)

    tm = min(TOKEN_TILE, n)
    tn_in = min(IN_PROJ_COLS, lr0, 3 * sb_w)
    tn_out = min(OUT_PROJ_COLS, d_model)
    tn_down = min(FFN_COLS, d_model)
    mixer_rows = min(MIXER_ROWS, t)
    xg, ss = _prenorm(x, g_mix[0], min(PRENORM_ROWS, n))
    for l in range(depth):
        proj_gla, lr, w_g = _matmul_scaled_with_gate(xg, ss, w_in_bf, l, lr0, lr0, tm, tn_in,
                                                     w_ffn_gate, f_main)
        proj_sb, w_u = _matmul_scaled(xg, ss, w_sb, l, 3 * sb_w, tm, tn_in,
                                      w_ffn_up, f_main, "in_proj_sb")
        o_a, w_o = _gla(proj_gla.reshape(bsz, t, lr0), lr.reshape(bsz, t, LANES), w_up[l],
                        b_gate[l].reshape(1, -1), gla_o_norm[l].reshape(1, -1), w_out, l,
                        heads=GLA_HEADS, dk=gla_dk, dv=gla_dv, rows=mixer_rows)
        o_b, w_d = _stick_breaking(proj_sb.reshape(bsz, t, 3 * sb_w), sb_q_norm[l].reshape(1, -1),
                                   sb_k_norm[l].reshape(1, -1), w_ffn_down, l, heads=sb_heads,
                                   d=SB_HEAD_DIM, col0=0, tq=mixer_rows, group=SB_HEAD_GROUP)
        x, xg, ss = _matmul2_residual(o_a.reshape(n, gla_w), o_b.reshape(n, sb_w), w_o, 0, x,
                                      g_ffn[l], tm, tn_out)

        u = _swiglu(xg, ss, w_g, w_u, 0, 0, 0, f_main, f_hidden, tm, tn_ffn)
        if f_main < f_hidden:
            f_tail = f_hidden - f_main
            u = _swiglu(xg, ss, w_g_tail, w_u_tail, l, 0, f_main, f_tail, f_hidden, tm, f_tail, into=u)
        for part in range(FFN_DOWN_PARTS - 1):
            (x,) = _down_residual(u, w_d, 0, part, FFN_DOWN_PARTS, x, None, tm, tn_down)
        g_next = g_mix[l + 1] if l + 1 < depth else None
        x, *norm_next = _down_residual(u, w_d, 0, FFN_DOWN_PARTS - 1, FFN_DOWN_PARTS, x, g_next, tm, tn_down)
        if norm_next:
            xg, ss = norm_next
    return x.reshape(bsz, t, d_model).astype(out_dtype)
```
